```python
import jax, jax.numpy as jnp
from jax import lax
import numpy as np

D_MODEL = 4096
BATCH = 4
SEQ = 2048
DEPTH = 1

D_MIX = D_MODEL
HEAD_DIM = 128
D_A = D_MIX // 2
D_B = D_MIX - D_A
N_HEADS_A = D_A // HEAD_DIM
N_BLOCKS_B = D_B // HEAD_DIM
CHUNK = 128
CONV_WIDTH = 4
LRU_C = 8.0
D_FF = ((8 * D_MODEL + 3 * 256 - 1) // (3 * 256)) * 256
D_IN = 2 * D_A + 2 * D_B
EPS = 1e-6

kernel_name = "hybrid_gmlp_rglru_sandwich_block"


def rmsnorm(x, g):
    x32 = x.astype(jnp.float32)
    y = x32 * lax.rsqrt(jnp.mean(x32 * x32, axis=-1, keepdims=True) + EPS)
    return (y * g.astype(jnp.float32)).astype(x.dtype)


def gmlp_mixer(u, v, v_norm_g, w_spatial, b_spatial):
    B, S, _ = v.shape
    u = jax.nn.gelu(u)
    v = jax.nn.gelu(v).reshape(B, S, N_HEADS_A, HEAD_DIM)
    v = rmsnorm(v, v_norm_g.reshape(N_HEADS_A, HEAD_DIM))
    vc = v.reshape(B, S // CHUNK, CHUNK, N_HEADS_A, HEAD_DIM)
    causal = jnp.tril(jnp.ones((CHUNK, CHUNK), dtype=bool))
    ws = jnp.where(causal[None], w_spatial, jnp.zeros_like(w_spatial))
    mixed = jnp.einsum('hts,bcshd->bcthd', ws, vc) + b_spatial.T[None, None, :, :, None]
    return u * mixed.reshape(B, S, D_A)


def causal_depthwise_conv(x, w_conv, b_conv):
    S = x.shape[1]
    xpad = jnp.pad(x, ((0, 0), (CONV_WIDTH - 1, 0), (0, 0)))
    out = b_conv
    for k in range(CONV_WIDTH):
        out = out + xpad[:, k:k + S, :] * w_conv[k]
    return out


def block_diag_linear(x, w, b):
    B, S, _ = x.shape
    xb = x.reshape(B, S, N_BLOCKS_B, HEAD_DIM)
    return jnp.einsum('bsnd,nde->bsne', xb, w).reshape(B, S, D_B) + b


def rglru_mixer(gate, xr, w_conv, b_conv, w_r, b_r, w_i, b_i, lru_lambda):
    xc = causal_depthwise_conv(xr, w_conv, b_conv)
    r = jax.nn.sigmoid(block_diag_linear(xc, w_r, b_r)).astype(jnp.float32)
    i = jax.nn.sigmoid(block_diag_linear(xc, w_i, b_i)).astype(jnp.float32)
    log_a = -LRU_C * r * jax.nn.softplus(-lru_lambda.astype(jnp.float32))
    a = jnp.exp(log_a)
    mult = jnp.sqrt(jnp.maximum(1.0 - jnp.exp(2.0 * log_a), 1e-12))
    bterm = mult * (i * xc.astype(jnp.float32))

    def combine(left, right):
        a_l, b_l = left
        a_r, b_r_ = right
        return a_l * a_r, a_r * b_l + b_r_

    _, h = lax.associative_scan(combine, (a, bterm), axis=1)
    return h.astype(xr.dtype) * jax.nn.gelu(gate)


def setup_inputs(seed: int = 0) -> dict:
    key = jax.random.key(seed)
    ks = jax.random.split(key, 24)
    f32 = jnp.float32

    def nrm(k, shape, scale):
        return jax.random.normal(k, shape, f32) * scale

    def gain(k, shape):
        return 1.0 + 0.02 * jax.random.normal(k, shape, f32)

    L = DEPTH
    x = jax.random.normal(ks[0], (BATCH, SEQ, D_MODEL), f32)
    pre_mix_g = gain(ks[1], (L, D_MODEL))
    w_in = nrm(ks[2], (L, D_MODEL, D_IN), D_MODEL ** -0.5)
    gmlp_v_norm_g = gain(ks[3], (L, D_A))
    w_spatial = nrm(ks[4], (L, N_HEADS_A, CHUNK, CHUNK), CHUNK ** -0.5)
    b_spatial = 1.0 + 0.01 * jax.random.normal(ks[5], (L, N_HEADS_A, CHUNK), f32)
    w_conv = nrm(ks[6], (L, CONV_WIDTH, D_B), CONV_WIDTH ** -0.5)
    b_conv = nrm(ks[7], (L, D_B), 0.01)
    w_r = nrm(ks[8], (L, N_BLOCKS_B, HEAD_DIM, HEAD_DIM), HEAD_DIM ** -0.5)
    b_r = nrm(ks[9], (L, D_B), 0.01)
    w_i = nrm(ks[10], (L, N_BLOCKS_B, HEAD_DIM, HEAD_DIM), HEAD_DIM ** -0.5)
    b_i = nrm(ks[11], (L, D_B), 0.01)
    a_c = jax.random.uniform(ks[12], (L, D_B), f32, 0.9, 0.999)
    a0 = a_c ** (1.0 / LRU_C)
    lru_lambda = jnp.log(a0) - jnp.log1p(-a0)
    out_norm_a_g = gain(ks[13], (L, D_A))
    out_norm_b_g = gain(ks[14], (L, D_B))
    w_out = nrm(ks[15], (L, D_MIX, D_MODEL), D_MIX ** -0.5)
    post_mix_g = gain(ks[16], (L, D_MODEL))
    pre_ffn_g = gain(ks[17], (L, D_MODEL))
    w_ffn_in = nrm(ks[18], (L, D_MODEL, 2 * D_FF), D_MODEL ** -0.5)
    w_ffn_out = nrm(ks[19], (L, D_FF, D_MODEL), D_FF ** -0.5)
    post_ffn_g = gain(ks[20], (L, D_MODEL))
    return {"x": x, "pre_mix_g": pre_mix_g, "w_in": w_in, "gmlp_v_norm_g": gmlp_v_norm_g,
            "w_spatial": w_spatial, "b_spatial": b_spatial, "w_conv": w_conv, "b_conv": b_conv,
            "w_r": w_r, "b_r": b_r, "w_i": w_i, "b_i": b_i, "lru_lambda": lru_lambda,
            "out_norm_a_g": out_norm_a_g, "out_norm_b_g": out_norm_b_g, "w_out": w_out,
            "post_mix_g": post_mix_g, "pre_ffn_g": pre_ffn_g, "w_ffn_in": w_ffn_in,
            "w_ffn_out": w_ffn_out, "post_ffn_g": post_ffn_g}


def reference(x, pre_mix_g, w_in, gmlp_v_norm_g, w_spatial, b_spatial, w_conv, b_conv,
              w_r, b_r, w_i, b_i, lru_lambda, out_norm_a_g, out_norm_b_g, w_out,
              post_mix_g, pre_ffn_g, w_ffn_in, w_ffn_out, post_ffn_g):
    for l in range(DEPTH):
        h = rmsnorm(x, pre_mix_g[l])
        proj = jnp.einsum('bsd,de->bse', h, w_in[l])
        u, v, gate, xr = jnp.split(proj, [D_A, 2 * D_A, 2 * D_A + D_B], axis=-1)
        y_a = gmlp_mixer(u, v, gmlp_v_norm_g[l], w_spatial[l], b_spatial[l])
        y_b = rglru_mixer(gate, xr, w_conv[l], b_conv[l], w_r[l], b_r[l],
                          w_i[l], b_i[l], lru_lambda[l])
        y = jnp.concatenate([rmsnorm(y_a, out_norm_a_g[l]),
                             rmsnorm(y_b, out_norm_b_g[l])], axis=-1)
        y = jnp.einsum('bse,ed->bsd', y, w_out[l])
        x = x + rmsnorm(y, post_mix_g[l])
        h = rmsnorm(x, pre_ffn_g[l])
        gu = jnp.einsum('bsd,df->bsf', h, w_ffn_in[l])
        g, up = jnp.split(gu, 2, axis=-1)
        f = jnp.einsum('bsf,fd->bsd', jax.nn.silu(g) * up, w_ffn_out[l])
        x = x + rmsnorm(f, post_ffn_g[l])
    return x
```

```python
from functools import partial

import jax
import jax.numpy as jnp
from jax import lax
from jax.experimental import pallas as pl
from jax.experimental.pallas import tpu as pltpu

EPS = 1e-6
HEAD_DIM = 128
CHUNK = 128
CONV_WIDTH = 4
LRU_C = 8.0
SUBLANES = 8
ROW_CHUNK = 32
VMEM_LIMIT_BYTES = 56 * 1024 * 1024

_BF16 = jnp.bfloat16
_F32 = jnp.float32


def _rms_scale(x):
    return lax.rsqrt(jnp.mean(x * x, axis=-1, keepdims=True) + EPS)


def _for_row_chunks(n_rows, body):
    def step(c, carry):
        body(pl.ds(pl.multiple_of(c * ROW_CHUNK, ROW_CHUNK), ROW_CHUNK))
        return carry
    lax.fori_loop(0, n_rows // ROW_CHUNK, step, 0)


def _params(semantics):
    return pltpu.CompilerParams(dimension_semantics=semantics,
                                vmem_limit_bytes=VMEM_LIMIT_BYTES)


def _norm_matmul_kernel(x_ref, g_ref, w_ref, o_ref, hn_ref):
    @pl.when(pl.program_id(1) == 0)
    def _():
        def norm_rows(rows):
            x = x_ref[rows, :]
            hn_ref[rows, :] = (x * _rms_scale(x) * g_ref[...]).astype(_BF16)
        _for_row_chunks(x_ref.shape[0], norm_rows)

    o_ref[...] = jnp.dot(hn_ref[...], w_ref[...], preferred_element_type=_F32)


def _norm_matmul(x, g, w, *, bm, bn):
    t, d = x.shape
    n = w.shape[1]
    return pl.pallas_call(
        _norm_matmul_kernel,
        grid=(t // bm, n // bn),
        in_specs=[pl.BlockSpec((bm, d), lambda i, j: (i, 0)),
                  pl.BlockSpec((1, d), lambda i, j: (0, 0)),
                  pl.BlockSpec((d, bn), lambda i, j: (0, j))],
        out_specs=pl.BlockSpec((bm, bn), lambda i, j: (i, j)),
        out_shape=jax.ShapeDtypeStruct((t, n), _F32),
        scratch_shapes=[pltpu.VMEM((bm, d), _BF16)],
        compiler_params=_params(("arbitrary", "arbitrary")),
        name="norm_matmul_in",
    )(x, g, w)


def _scan_rows(a, b, h0):
    tm = a.shape[0]
    row = lax.broadcasted_iota(jnp.int32, a.shape, 0) % SUBLANES
    d = 1
    while d < SUBLANES:
        keep = row >= d
        a_sh = pltpu.roll(a, d, axis=0)
        b_sh = pltpu.roll(b, d, axis=0)
        b = jnp.where(keep, a * b_sh + b, b)
        a = jnp.where(keep, a * a_sh, a)
        d *= 2
    hs = []
    carry = h0
    for g in range(tm // SUBLANES):
        sl = slice(g * SUBLANES, (g + 1) * SUBLANES)
        h = a[sl] * carry + b[sl]
        hs.append(h)
        carry = jnp.broadcast_to(h[SUBLANES - 1:SUBLANES, :], h.shape)
    return jnp.concatenate(hs, axis=0), carry


def _mixer_kernel(u_ref, v_ref, gate_ref, xr_ref, vg_ref, ws_ref, bs_ref,
                  wc_ref, bc_ref, wr_ref, br_ref, wi_ref, bi_ref, lam_ref,
                  ga_ref, gb_ref, o_ref, ya_ref, yb_ref, xe_ref, carry_ref):
    tm, da = u_ref.shape
    db = xr_ref.shape[1]
    hist = SUBLANES

    @pl.when(pl.program_id(1) == 0)
    def _():
        xe_ref[0:hist, :] = jnp.zeros((hist, db), _F32)
        carry_ref[...] = jnp.zeros(carry_ref.shape, _F32)

    tril = (lax.broadcasted_iota(jnp.int32, (CHUNK, CHUNK), 0)
            >= lax.broadcasted_iota(jnp.int32, (CHUNK, CHUNK), 1))
    for h in range(da // HEAD_DIM):
        cols = slice(h * HEAD_DIM, (h + 1) * HEAD_DIM)
        gv = jax.nn.gelu(v_ref[:, cols])
        vn = (gv * _rms_scale(gv) * vg_ref[:, cols]).astype(_BF16)
        ws = jnp.where(tril, ws_ref[h], 0.0).astype(_BF16)
        bias = jnp.broadcast_to(bs_ref[:, h:h + 1], (CHUNK, HEAD_DIM))
        for c in range(tm // CHUNK):
            rows = slice(c * CHUNK, (c + 1) * CHUNK)
            mixed = jnp.dot(ws, vn[rows], preferred_element_type=_F32) + bias
            ya_ref[rows, cols] = jax.nn.gelu(u_ref[rows, cols]) * mixed

    xe_ref[hist:hist + tm, :] = xr_ref[...]
    lam = lam_ref[...]
    neg = -lam
    softplus = jnp.maximum(neg, 0.0) + jnp.log1p(jnp.exp(-jnp.abs(neg)))
    for n in range(db // HEAD_DIM):
        cols = slice(n * HEAD_DIM, (n + 1) * HEAD_DIM)
        xc = bc_ref[:, cols]
        for k in range(CONV_WIDTH):
            tap = xe_ref[pl.ds(hist - (CONV_WIDTH - 1) + k, tm), cols]
            xc = xc + tap * wc_ref[k:k + 1, cols]
        xc16 = xc.astype(_BF16)
        r = jax.nn.sigmoid(jnp.dot(xc16, wr_ref[n], preferred_element_type=_F32) + br_ref[:, cols])
        i = jax.nn.sigmoid(jnp.dot(xc16, wi_ref[n], preferred_element_type=_F32) + bi_ref[:, cols])
        log_a = -LRU_C * r * softplus[:, cols]
        a = jnp.exp(log_a)
        mult = jnp.sqrt(jnp.maximum(1.0 - a * a, 1e-12))
        bterm = mult * (i * xc)
        hseq, carry = _scan_rows(a, bterm, carry_ref[:, cols])
        carry_ref[:, cols] = carry
        yb_ref[:, cols] = hseq * jax.nn.gelu(gate_ref[:, cols])
    xe_ref[0:hist, :] = xe_ref[tm:tm + hist, :]

    def norm_rows(rows):
        ya = ya_ref[rows, :]
        o_ref[rows, 0:da] = (ya * _rms_scale(ya) * ga_ref[...]).astype(o_ref.dtype)
        yb = yb_ref[rows, :]
        o_ref[rows, da:da + db] = (yb * _rms_scale(yb) * gb_ref[...]).astype(o_ref.dtype)
    _for_row_chunks(tm, norm_rows)


def _mixers(proj, vg, ws, bs_t, wc, bc, wr, br, wi, bi, lam, ga, gb, *, batch, seq, tm):
    t = proj.shape[0]
    da = vg.shape[1]
    db = lam.shape[1]
    ns = seq // tm
    nh = ws.shape[0]
    nb = wr.shape[0]

    def col_block(c):
        return pl.BlockSpec((tm, da), lambda b, s, c=c: (b * ns + s, c))

    def whole(shape):
        return pl.BlockSpec(shape, lambda b, s: (0,) * len(shape))

    return pl.pallas_call(
        _mixer_kernel,
        grid=(batch, ns),
        in_specs=[col_block(0), col_block(1), col_block(2), col_block(3),
                  whole((1, da)), whole((nh, CHUNK, CHUNK)), whole((CHUNK, nh)),
                  whole((CONV_WIDTH, db)), whole((1, db)),
                  whole((nb, HEAD_DIM, HEAD_DIM)), whole((1, db)),
                  whole((nb, HEAD_DIM, HEAD_DIM)), whole((1, db)), whole((1, db)),
                  whole((1, da)), whole((1, db))],
        out_specs=pl.BlockSpec((tm, da + db), lambda b, s: (b * ns + s, 0)),
        out_shape=jax.ShapeDtypeStruct((t, da + db), _BF16),
        scratch_shapes=[pltpu.VMEM((tm, da), _F32),
                        pltpu.VMEM((tm, db), _F32),
                        pltpu.VMEM((tm + 2 * SUBLANES, db), _F32),
                        pltpu.VMEM((SUBLANES, db), _F32)],
        compiler_params=_params(("arbitrary", "arbitrary")),
        name="token_mixers",
    )(proj, proj, proj, proj, vg, ws, bs_t, wc, bc, wr, br, wi, bi, lam, ga, gb)


def _matmul_kernel(a_ref, w_ref, o_ref):
    o_ref[...] = jnp.dot(a_ref[...], w_ref[...], preferred_element_type=_F32)


def _matmul(a, w, *, bm, bn):
    t, k = a.shape
    n = w.shape[1]
    return pl.pallas_call(
        _matmul_kernel,
        grid=(t // bm, n // bn),
        in_specs=[pl.BlockSpec((bm, k), lambda i, j: (i, 0)),
                  pl.BlockSpec((k, bn), lambda i, j: (0, j))],
        out_specs=pl.BlockSpec((bm, bn), lambda i, j: (i, j)),
        out_shape=jax.ShapeDtypeStruct((t, n), _F32),
        compiler_params=_params(("arbitrary", "arbitrary")),
        name="matmul_out",
    )(a, w)


def _resid_norm_kernel(x_ref, y_ref, gpost_ref, gpre_ref, x1_ref, h2_ref):
    def rows_step(rows):
        y = y_ref[rows, :]
        x1 = x_ref[rows, :] + y * _rms_scale(y) * gpost_ref[...]
        x1_ref[rows, :] = x1
        h2_ref[rows, :] = (x1 * _rms_scale(x1) * gpre_ref[...]).astype(h2_ref.dtype)
    _for_row_chunks(x_ref.shape[0], rows_step)


def _resid_norm(x, y, gpost, gpre, *, bm):
    t, d = x.shape
    row = pl.BlockSpec((bm, d), lambda i: (i, 0))
    vec = pl.BlockSpec((1, d), lambda i: (0, 0))
    return pl.pallas_call(
        _resid_norm_kernel,
        grid=(t // bm,),
        in_specs=[row, row, vec, vec],
        out_specs=[row, row],
        out_shape=[jax.ShapeDtypeStruct((t, d), _F32), jax.ShapeDtypeStruct((t, d), _BF16)],
        compiler_params=_params(("arbitrary",)),
        name="resid_norm",
    )(x, y, gpost, gpre)


def _ffn_kernel(h_ref, x1_ref, wg_ref, wu_ref, wo_ref, g_ref, o_ref):
    j = pl.program_id(1)
    h = h_ref[...]
    gate = jnp.dot(h, wg_ref[...], preferred_element_type=_F32)
    up = jnp.dot(h, wu_ref[...], preferred_element_type=_F32)
    act = (jax.nn.silu(gate) * up).astype(_BF16)

    @pl.when(j == 0)
    def _():
        o_ref[...] = jnp.dot(act, wo_ref[...], preferred_element_type=_F32)

    @pl.when(j > 0)
    def _():
        o_ref[...] += jnp.dot(act, wo_ref[...], preferred_element_type=_F32)

    @pl.when(j == pl.num_programs(1) - 1)
    def _():
        def rows_step(rows):
            f = o_ref[rows, :]
            o_ref[rows, :] = x1_ref[rows, :] + f * _rms_scale(f) * g_ref[...]
        _for_row_chunks(o_ref.shape[0], rows_step)


def _ffn(h2, x1, w_in, w_out, g, *, bm, bf):
    t, d = h2.shape
    f = w_out.shape[0]
    nf = f // bf
    return pl.pallas_call(
        _ffn_kernel,
        grid=(t // bm, nf),
        in_specs=[pl.BlockSpec((bm, d), lambda i, j: (i, 0)),
                  pl.BlockSpec((bm, d), lambda i, j: (i, 0)),
                  pl.BlockSpec((d, bf), lambda i, j: (0, j)),
                  pl.BlockSpec((d, bf), lambda i, j: (0, j + nf)),
                  pl.BlockSpec((bf, d), lambda i, j: (j, 0)),
                  pl.BlockSpec((1, d), lambda i, j: (0, 0))],
        out_specs=pl.BlockSpec((bm, d), lambda i, j: (i, 0)),
        out_shape=jax.ShapeDtypeStruct((t, d), _F32),
        compiler_params=_params(("arbitrary", "arbitrary")),
        name="swiglu_ffn",
    )(h2, x1, w_in, w_in, w_out, g)


def kernel(x, pre_mix_g, w_in, gmlp_v_norm_g, w_spatial, b_spatial, w_conv, b_conv, w_r, b_r, w_i, b_i, lru_lambda, out_norm_a_g, out_norm_b_g, w_out, post_mix_g, pre_ffn_g, w_ffn_in, w_ffn_out, post_ffn_g):
    batch, seq, d = x.shape
    depth = w_in.shape[0]
    xt = x.reshape(batch * seq, d)
    for l in range(depth):
        proj = _norm_matmul(xt, pre_mix_g[l][None], w_in[l].astype(_BF16), bm=512, bn=1024)
        yn = _mixers(proj, gmlp_v_norm_g[l][None], w_spatial[l], b_spatial[l].T,
                     w_conv[l], b_conv[l][None], w_r[l].astype(_BF16), b_r[l][None],
                     w_i[l].astype(_BF16), b_i[l][None], lru_lambda[l][None],
                     out_norm_a_g[l][None], out_norm_b_g[l][None],
                     batch=batch, seq=seq, tm=256)
        y2 = _matmul(yn, w_out[l].astype(_BF16), bm=1024, bn=1024)
        x1, h2 = _resid_norm(xt, y2, post_mix_g[l][None], pre_ffn_g[l][None], bm=256)
        xt = _ffn(h2, x1, w_ffn_in[l].astype(_BF16), w_ffn_out[l].astype(_BF16),
                  post_ffn_g[l][None], bm=512, bf=256)
    return xt.reshape(batch, seq, d)
```

```python
import math
from functools import partial

import jax
import jax.numpy as jnp
from jax import lax
from jax.experimental import pallas as pl
from jax.experimental.pallas import tpu as pltpu

EPS = 1e-6
HEAD_DIM = 128
CHUNK = 128
CONV_WIDTH = 4
LRU_C = 8.0
SUBLANES = 8
ROW_CHUNK = 32
VMEM_LIMIT_BYTES = 60 * 1024 * 1024

_BF16 = jnp.bfloat16
_F32 = jnp.float32
_LOG2E = math.log2(math.e)
_GELU_A = -2.0 * math.sqrt(2.0 / math.pi) * _LOG2E
_GELU_B = _GELU_A * 0.044715


def _gelu(x):
    return x / (1.0 + jnp.exp2(x * (_GELU_A + _GELU_B * (x * x))))


def _rms_scale(x):
    return lax.rsqrt(jnp.mean(x * x, axis=-1, keepdims=True) + EPS)


def _for_row_chunks(n_rows, body):
    def step(c, carry):
        body(pl.ds(pl.multiple_of(c * ROW_CHUNK, ROW_CHUNK), ROW_CHUNK))
        return carry
    lax.fori_loop(0, n_rows // ROW_CHUNK, step, 0)


def _params(semantics):
    return pltpu.CompilerParams(dimension_semantics=semantics,
                                vmem_limit_bytes=VMEM_LIMIT_BYTES)


_SINGLE = pl.Buffered(1)


def _norm_matmul_kernel(x_ref, g_ref, w_ref, o_ref, hn_ref):
    @pl.when(pl.program_id(1) == 0)
    def _():
        def norm_rows(rows):
            x = x_ref[rows, :]
            hn_ref[rows, :] = (x * _rms_scale(x) * g_ref[...]).astype(_BF16)
        _for_row_chunks(x_ref.shape[0], norm_rows)

    o_ref[...] = jnp.dot(hn_ref[...], w_ref[...], preferred_element_type=_F32)


def _norm_matmul(x, g, w, *, bm, bn):
    t, d = x.shape
    n = w.shape[1]
    return pl.pallas_call(
        _norm_matmul_kernel,
        grid=(t // bm, n // bn),
        in_specs=[pl.BlockSpec((bm, d), lambda i, j: (i, 0), pipeline_mode=_SINGLE),
                  pl.BlockSpec((1, d), lambda i, j: (0, 0)),
                  pl.BlockSpec((d, bn), lambda i, j: (0, j))],
        out_specs=pl.BlockSpec((bm, bn), lambda i, j: (i, j)),
        out_shape=jax.ShapeDtypeStruct((t, n), _F32),
        scratch_shapes=[pltpu.VMEM((bm, d), _BF16)],
        compiler_params=_params(("arbitrary", "arbitrary")),
        name="norm_matmul_in",
    )(x, g, w)


def _scan_rows(a, b, carry):
    tm = a.shape[0]
    sub = lax.broadcasted_iota(jnp.int32, (SUBLANES, a.shape[1]), 0)
    hs = []
    for g in range(tm // SUBLANES):
        sl = slice(g * SUBLANES, (g + 1) * SUBLANES)
        ag, bg = a[sl], b[sl]
        d = 1
        while d < SUBLANES:
            keep = sub >= d
            a_sh = jnp.where(keep, pltpu.roll(ag, d, axis=0), 1.0)
            b_sh = jnp.where(keep, pltpu.roll(bg, d, axis=0), 0.0)
            bg = ag * b_sh + bg
            ag = ag * a_sh
            d *= 2
        h = ag * carry + bg
        hs.append(h)
        carry = jnp.broadcast_to(h[SUBLANES - 1:SUBLANES, :], h.shape)
    return jnp.concatenate(hs, axis=0), carry


def _mixer_kernel(u_ref, v_ref, gate_ref, xr_ref, vg_ref, ws_ref, bs_ref,
                  wc_ref, bc_ref, wr_ref, br_ref, wi_ref, bi_ref, lam_ref,
                  ga_ref, gb_ref, o_ref, ya_ref, yb_ref, xe_ref, carry_ref):
    tm, da = u_ref.shape
    db = xr_ref.shape[1]
    hist = SUBLANES

    @pl.when(pl.program_id(1) == 0)
    def _():
        xe_ref[0:hist, :] = jnp.zeros((hist, db), _F32)
        carry_ref[...] = jnp.zeros(carry_ref.shape, _F32)

    tril = (lax.broadcasted_iota(jnp.int32, (CHUNK, CHUNK), 0)
            >= lax.broadcasted_iota(jnp.int32, (CHUNK, CHUNK), 1))
    for h in range(da // HEAD_DIM):
        cols = slice(h * HEAD_DIM, (h + 1) * HEAD_DIM)
        gv = _gelu(v_ref[:, cols])
        vn = (gv * _rms_scale(gv) * vg_ref[:, cols]).astype(_BF16)
        ws = jnp.where(tril, ws_ref[h], 0.0).astype(_BF16)
        bias = jnp.broadcast_to(bs_ref[:, h:h + 1], (CHUNK, HEAD_DIM))
        for c in range(tm // CHUNK):
            rows = slice(c * CHUNK, (c + 1) * CHUNK)
            mixed = jnp.dot(ws, vn[rows], preferred_element_type=_F32) + bias
            ya_ref[rows, cols] = _gelu(u_ref[rows, cols]) * mixed

    xe_ref[hist:hist + tm, :] = xr_ref[...]
    neg = -lam_ref[...]
    softplus = jnp.maximum(neg, 0.0) + jnp.log1p(jnp.exp(-jnp.abs(neg)))
    log2_a_per_r = (-LRU_C * _LOG2E) * softplus
    for n in range(db // HEAD_DIM):
        cols = slice(n * HEAD_DIM, (n + 1) * HEAD_DIM)
        xc = bc_ref[:, cols]
        for k in range(CONV_WIDTH):
            tap = xe_ref[pl.ds(hist - (CONV_WIDTH - 1) + k, tm), cols]
            xc = xc + tap * wc_ref[k:k + 1, cols]
        xc16 = xc.astype(_BF16)
        r = jax.nn.sigmoid(jnp.dot(xc16, wr_ref[n], preferred_element_type=_F32) + br_ref[:, cols])
        i = jax.nn.sigmoid(jnp.dot(xc16, wi_ref[n], preferred_element_type=_F32) + bi_ref[:, cols])
        a = jnp.exp2(r * log2_a_per_r[:, cols])
        m2 = jnp.maximum(1.0 - a * a, 1e-12)
        bterm = (m2 * lax.rsqrt(m2)) * (i * xc)
        hseq, carry = _scan_rows(a, bterm, carry_ref[:, cols])
        carry_ref[:, cols] = carry
        yb_ref[:, cols] = hseq * _gelu(gate_ref[:, cols])
    xe_ref[0:hist, :] = xe_ref[tm:tm + hist, :]

    def norm_rows(rows):
        ya = ya_ref[rows, :]
        o_ref[rows, 0:da] = (ya * _rms_scale(ya) * ga_ref[...]).astype(o_ref.dtype)
        yb = yb_ref[rows, :]
        o_ref[rows, da:da + db] = (yb * _rms_scale(yb) * gb_ref[...]).astype(o_ref.dtype)
    _for_row_chunks(tm, norm_rows)


def _mixers(proj, vg, ws, bs_t, wc, bc, wr, br, wi, bi, lam, ga, gb, *, batch, seq, tm):
    t = proj.shape[0]
    da = vg.shape[1]
    db = lam.shape[1]
    ns = seq // tm
    nh = ws.shape[0]
    nb = wr.shape[0]

    def col_block(c):
        return pl.BlockSpec((tm, da), lambda b, s, c=c: (b * ns + s, c))

    def whole(shape):
        return pl.BlockSpec(shape, lambda b, s: (0,) * len(shape))

    return pl.pallas_call(
        _mixer_kernel,
        grid=(batch, ns),
        in_specs=[col_block(0), col_block(1), col_block(2), col_block(3),
                  whole((1, da)), whole((nh, CHUNK, CHUNK)), whole((CHUNK, nh)),
                  whole((CONV_WIDTH, db)), whole((1, db)),
                  whole((nb, HEAD_DIM, HEAD_DIM)), whole((1, db)),
                  whole((nb, HEAD_DIM, HEAD_DIM)), whole((1, db)), whole((1, db)),
                  whole((1, da)), whole((1, db))],
        out_specs=pl.BlockSpec((tm, da + db), lambda b, s: (b * ns + s, 0)),
        out_shape=jax.ShapeDtypeStruct((t, da + db), _BF16),
        scratch_shapes=[pltpu.VMEM((tm, da), _F32),
                        pltpu.VMEM((tm, db), _F32),
                        pltpu.VMEM((tm + 2 * SUBLANES, db), _F32),
                        pltpu.VMEM((SUBLANES, db), _F32)],
        compiler_params=_params(("arbitrary", "arbitrary")),
        name="token_mixers",
    )(proj, proj, proj, proj, vg, ws, bs_t, wc, bc, wr, br, wi, bi, lam, ga, gb)


def _out_proj_kernel(yn_ref, w_ref, x_ref, g_ref, x1_ref, *, bn):
    yn = yn_ref[...]
    for n in range(w_ref.shape[1] // bn):
        cols = slice(n * bn, (n + 1) * bn)
        x1_ref[:, cols] = jnp.dot(yn, w_ref[:, cols], preferred_element_type=_F32)

    def rows_step(rows):
        y = x1_ref[rows, :]
        x1_ref[rows, :] = x_ref[rows, :] + y * _rms_scale(y) * g_ref[...]
    _for_row_chunks(x1_ref.shape[0], rows_step)


def _out_proj(yn, w, x, g, *, bm, bn):
    t, k = yn.shape
    d = w.shape[1]
    row = lambda i: (i, 0)
    const = lambda i: (0, 0)
    return pl.pallas_call(
        partial(_out_proj_kernel, bn=bn),
        grid=(t // bm,),
        in_specs=[pl.BlockSpec((bm, k), row),
                  pl.BlockSpec((k, d), const),
                  pl.BlockSpec((bm, d), row),
                  pl.BlockSpec((1, d), const)],
        out_specs=pl.BlockSpec((bm, d), row),
        out_shape=jax.ShapeDtypeStruct((t, d), _F32),
        compiler_params=_params(("arbitrary",)),
        name="out_proj_resid",
    )(yn, w, x, g)


def _ffn_kernel(x1_ref, gpre_ref, wg_ref, wu_ref, wo_ref, gpost_ref, o_ref, h_ref):
    j = pl.program_id(1)

    @pl.when(j == 0)
    def _():
        def norm_rows(rows):
            x1 = x1_ref[rows, :]
            h_ref[rows, :] = (x1 * _rms_scale(x1) * gpre_ref[...]).astype(_BF16)
        _for_row_chunks(x1_ref.shape[0], norm_rows)

    h = h_ref[...]
    gate = jnp.dot(h, wg_ref[...], preferred_element_type=_F32)
    up = jnp.dot(h, wu_ref[...], preferred_element_type=_F32)
    act = (jax.nn.silu(gate) * up).astype(_BF16)

    @pl.when(j == 0)
    def _():
        o_ref[...] = jnp.dot(act, wo_ref[...], preferred_element_type=_F32)

    @pl.when(j > 0)
    def _():
        o_ref[...] += jnp.dot(act, wo_ref[...], preferred_element_type=_F32)

    @pl.when(j == pl.num_programs(1) - 1)
    def _():
        def rows_step(rows):
            f = o_ref[rows, :]
            o_ref[rows, :] = x1_ref[rows, :] + f * _rms_scale(f) * gpost_ref[...]
        _for_row_chunks(o_ref.shape[0], rows_step)


def _ffn(x1, gpre, w_in, w_out, gpost, *, bm, bf):
    t, d = x1.shape
    f = w_out.shape[0]
    nf = f // bf
    return pl.pallas_call(
        _ffn_kernel,
        grid=(t // bm, nf),
        in_specs=[pl.BlockSpec((bm, d), lambda i, j: (i, 0), pipeline_mode=_SINGLE),
                  pl.BlockSpec((1, d), lambda i, j: (0, 0)),
                  pl.BlockSpec((d, bf), lambda i, j: (0, j)),
                  pl.BlockSpec((d, bf), lambda i, j: (0, j + nf)),
                  pl.BlockSpec((bf, d), lambda i, j: (j, 0)),
                  pl.BlockSpec((1, d), lambda i, j: (0, 0))],
        out_specs=pl.BlockSpec((bm, d), lambda i, j: (i, 0), pipeline_mode=_SINGLE),
        out_shape=jax.ShapeDtypeStruct((t, d), _F32),
        scratch_shapes=[pltpu.VMEM((bm, d), _BF16)],
        compiler_params=_params(("arbitrary", "arbitrary")),
        name="swiglu_ffn",
    )(x1, gpre, w_in, w_in, w_out, gpost)


def kernel(x, pre_mix_g, w_in, gmlp_v_norm_g, w_spatial, b_spatial, w_conv, b_conv, w_r, b_r, w_i, b_i, lru_lambda, out_norm_a_g, out_norm_b_g, w_out, post_mix_g, pre_ffn_g, w_ffn_in, w_ffn_out, post_ffn_g):
    batch, seq, d = x.shape
    depth = w_in.shape[0]
    xt = x.reshape(batch * seq, d)
    for l in range(depth):
        proj = _norm_matmul(xt, pre_mix_g[l][None], w_in[l].astype(_BF16), bm=1024, bn=1024)
        yn = _mixers(proj, gmlp_v_norm_g[l][None], w_spatial[l], b_spatial[l].T,
                     w_conv[l], b_conv[l][None], w_r[l].astype(_BF16), b_r[l][None],
                     w_i[l].astype(_BF16), b_i[l][None], lru_lambda[l][None],
                     out_norm_a_g[l][None], out_norm_b_g[l][None],
                     batch=batch, seq=seq, tm=256)
        x1 = _out_proj(yn, w_out[l].astype(_BF16), xt, post_mix_g[l][None], bm=256, bn=1024)
        xt = _ffn(x1, pre_ffn_g[l][None], w_ffn_in[l].astype(_BF16), w_ffn_out[l].astype(_BF16),
                  post_ffn_g[l][None], bm=1024, bf=256)
    return xt.reshape(batch, seq, d)
```

```python
import math
from functools import partial

import jax
import jax.numpy as jnp
from jax import lax
from jax.experimental import pallas as pl
from jax.experimental.pallas import tpu as pltpu

EPS = 1e-6
HEAD_DIM = 128
CHUNK = 128
CONV_WIDTH = 4
LRU_C = 8.0
SUBLANES = 8
ROW_CHUNK = 32
VMEM_LIMIT_BYTES = 60 * 1024 * 1024

_BF16 = jnp.bfloat16
_F32 = jnp.float32
_LOG2E = math.log2(math.e)
_GELU_A = -2.0 * math.sqrt(2.0 / math.pi) * _LOG2E
_GELU_B = _GELU_A * 0.044715


def _gelu(x):
    return x / (1.0 + jnp.exp2(x * (_GELU_A + _GELU_B * (x * x))))


def _rms_scale(x):
    return lax.rsqrt(jnp.mean(x * x, axis=-1, keepdims=True) + EPS)


def _for_row_chunks(n_rows, body):
    def step(c, carry):
        body(pl.ds(pl.multiple_of(c * ROW_CHUNK, ROW_CHUNK), ROW_CHUNK))
        return carry
    lax.fori_loop(0, n_rows // ROW_CHUNK, step, 0)


def _params(semantics):
    return pltpu.CompilerParams(dimension_semantics=semantics,
                                vmem_limit_bytes=VMEM_LIMIT_BYTES)


_SINGLE = pl.Buffered(1)


def _norm_matmul_kernel(x_ref, g_ref, w_ref, o_ref, hn_ref):
    @pl.when(pl.program_id(1) == 0)
    def _():
        def norm_rows(rows):
            x = x_ref[rows, :]
            hn_ref[rows, :] = (x * _rms_scale(x) * g_ref[...]).astype(_BF16)
        _for_row_chunks(x_ref.shape[0], norm_rows)

    o_ref[...] = jnp.dot(hn_ref[...], w_ref[...], preferred_element_type=_F32)


def _norm_matmul(x, g, w, *, bm, bn):
    t, d = x.shape
    n = w.shape[1]
    return pl.pallas_call(
        _norm_matmul_kernel,
        grid=(t // bm, n // bn),
        in_specs=[pl.BlockSpec((bm, d), lambda i, j: (i, 0), pipeline_mode=_SINGLE),
                  pl.BlockSpec((1, d), lambda i, j: (0, 0)),
                  pl.BlockSpec((d, bn), lambda i, j: (0, j))],
        out_specs=pl.BlockSpec((bm, bn), lambda i, j: (i, j)),
        out_shape=jax.ShapeDtypeStruct((t, n), _F32),
        scratch_shapes=[pltpu.VMEM((bm, d), _BF16)],
        compiler_params=_params(("arbitrary", "arbitrary")),
        name="norm_matmul_in",
    )(x, g, w)


def _scan_rows(a, b, carry):
    tm = a.shape[0]
    sub = lax.broadcasted_iota(jnp.int32, (SUBLANES, a.shape[1]), 0)
    hs = []
    for g in range(tm // SUBLANES):
        sl = slice(g * SUBLANES, (g + 1) * SUBLANES)
        ag, bg = a[sl], b[sl]
        d = 1
        while d < SUBLANES:
            keep = sub >= d
            a_sh = jnp.where(keep, pltpu.roll(ag, d, axis=0), 1.0)
            b_sh = jnp.where(keep, pltpu.roll(bg, d, axis=0), 0.0)
            bg = ag * b_sh + bg
            ag = ag * a_sh
            d *= 2
        h = ag * carry + bg
        hs.append(h)
        carry = jnp.broadcast_to(h[SUBLANES - 1:SUBLANES, :], h.shape)
    return jnp.concatenate(hs, axis=0), carry


def _mixer_kernel(u_ref, v_ref, gate_ref, xr_ref, vg_ref, ws_ref, bs_ref,
                  wc_ref, bc_ref, wr_ref, br_ref, wi_ref, bi_ref, lam_ref,
                  ga_ref, gb_ref, o_ref, ya_ref, yb_ref, xe_ref, carry_ref):
    tm, da = u_ref.shape
    db = xr_ref.shape[1]
    hist = SUBLANES

    @pl.when(pl.program_id(1) == 0)
    def _():
        xe_ref[0:hist, :] = jnp.zeros((hist, db), _F32)
        carry_ref[...] = jnp.zeros(carry_ref.shape, _F32)

    tril = (lax.broadcasted_iota(jnp.int32, (CHUNK, CHUNK), 0)
            >= lax.broadcasted_iota(jnp.int32, (CHUNK, CHUNK), 1))
    for h in range(da // HEAD_DIM):
        cols = slice(h * HEAD_DIM, (h + 1) * HEAD_DIM)
        gv = _gelu(v_ref[:, cols])
        vn = (gv * _rms_scale(gv) * vg_ref[:, cols]).astype(_BF16)
        ws = jnp.where(tril, ws_ref[h], 0.0).astype(_BF16)
        bias = jnp.broadcast_to(bs_ref[:, h:h + 1], (CHUNK, HEAD_DIM))
        for c in range(tm // CHUNK):
            rows = slice(c * CHUNK, (c + 1) * CHUNK)
            mixed = jnp.dot(ws, vn[rows], preferred_element_type=_F32) + bias
            ya_ref[rows, cols] = _gelu(u_ref[rows, cols]) * mixed

    xe_ref[hist:hist + tm, :] = xr_ref[...]
    neg = -lam_ref[...]
    softplus = jnp.maximum(neg, 0.0) + jnp.log1p(jnp.exp(-jnp.abs(neg)))
    log2_a_per_r = (-LRU_C * _LOG2E) * softplus
    for n in range(db // HEAD_DIM):
        cols = slice(n * HEAD_DIM, (n + 1) * HEAD_DIM)
        xc = bc_ref[:, cols]
        for k in range(CONV_WIDTH):
            tap = xe_ref[pl.ds(hist - (CONV_WIDTH - 1) + k, tm), cols]
            xc = xc + tap * wc_ref[k:k + 1, cols]
        xc16 = xc.astype(_BF16)
        r = jax.nn.sigmoid(jnp.dot(xc16, wr_ref[n], preferred_element_type=_F32) + br_ref[:, cols])
        i = jax.nn.sigmoid(jnp.dot(xc16, wi_ref[n], preferred_element_type=_F32) + bi_ref[:, cols])
        a = jnp.exp2(r * log2_a_per_r[:, cols])
        m2 = jnp.maximum(1.0 - a * a, 1e-12)
        bterm = (m2 * lax.rsqrt(m2)) * (i * xc)
        hseq, carry = _scan_rows(a, bterm, carry_ref[:, cols])
        carry_ref[:, cols] = carry
        yb_ref[:, cols] = hseq * _gelu(gate_ref[:, cols])
    xe_ref[0:hist, :] = xe_ref[tm:tm + hist, :]

    def norm_rows(rows):
        ya = ya_ref[rows, :]
        o_ref[rows, 0:da] = (ya * _rms_scale(ya) * ga_ref[...]).astype(o_ref.dtype)
        yb = yb_ref[rows, :]
        o_ref[rows, da:da + db] = (yb * _rms_scale(yb) * gb_ref[...]).astype(o_ref.dtype)
    _for_row_chunks(tm, norm_rows)


def _mixers(proj, vg, ws, bs_t, wc, bc, wr, br, wi, bi, lam, ga, gb, *, batch, seq, tm):
    t = proj.shape[0]
    da = vg.shape[1]
    db = lam.shape[1]
    ns = seq // tm
    nh = ws.shape[0]
    nb = wr.shape[0]

    def col_block(c):
        return pl.BlockSpec((tm, da), lambda b, s, c=c: (b * ns + s, c))

    def whole(shape):
        return pl.BlockSpec(shape, lambda b, s: (0,) * len(shape))

    return pl.pallas_call(
        _mixer_kernel,
        grid=(batch, ns),
        in_specs=[col_block(0), col_block(1), col_block(2), col_block(3),
                  whole((1, da)), whole((nh, CHUNK, CHUNK)), whole((CHUNK, nh)),
                  whole((CONV_WIDTH, db)), whole((1, db)),
                  whole((nb, HEAD_DIM, HEAD_DIM)), whole((1, db)),
                  whole((nb, HEAD_DIM, HEAD_DIM)), whole((1, db)), whole((1, db)),
                  whole((1, da)), whole((1, db))],
        out_specs=pl.BlockSpec((tm, da + db), lambda b, s: (b * ns + s, 0)),
        out_shape=jax.ShapeDtypeStruct((t, da + db), _BF16),
        scratch_shapes=[pltpu.VMEM((tm, da), _F32),
                        pltpu.VMEM((tm, db), _F32),
                        pltpu.VMEM((tm + 2 * SUBLANES, db), _F32),
                        pltpu.VMEM((SUBLANES, db), _F32)],
        compiler_params=_params(("arbitrary", "arbitrary")),
        name="token_mixers",
    )(proj, proj, proj, proj, vg, ws, bs_t, wc, bc, wr, br, wi, bi, lam, ga, gb)


def _out_proj_kernel(yn_ref, w_ref, x_ref, g_ref, x1_ref, *, bn):
    yn = yn_ref[...]
    for n in range(w_ref.shape[1] // bn):
        cols = slice(n * bn, (n + 1) * bn)
        x1_ref[:, cols] = jnp.dot(yn, w_ref[:, cols], preferred_element_type=_F32)

    def rows_step(rows):
        y = x1_ref[rows, :]
        x1_ref[rows, :] = x_ref[rows, :] + y * _rms_scale(y) * g_ref[...]
    _for_row_chunks(x1_ref.shape[0], rows_step)


def _out_proj(yn, w, x, g, *, bm, bn):
    t, k = yn.shape
    d = w.shape[1]
    row = lambda i: (i, 0)
    const = lambda i: (0, 0)
    return pl.pallas_call(
        partial(_out_proj_kernel, bn=bn),
        grid=(t // bm,),
        in_specs=[pl.BlockSpec((bm, k), row),
                  pl.BlockSpec((k, d), const),
                  pl.BlockSpec((bm, d), row),
                  pl.BlockSpec((1, d), const)],
        out_specs=pl.BlockSpec((bm, d), row),
        out_shape=jax.ShapeDtypeStruct((t, d), _F32),
        compiler_params=_params(("arbitrary",)),
        name="out_proj_resid",
    )(yn, w, x, g)


def _ffn_kernel(x1_ref, gpre_ref, wg_ref, wu_ref, wo_ref, gpost_ref, o_ref, h_ref):
    j = pl.program_id(1)

    @pl.when(j == 0)
    def _():
        def norm_rows(rows):
            x1 = x1_ref[rows, :]
            h_ref[rows, :] = (x1 * _rms_scale(x1) * gpre_ref[...]).astype(_BF16)
        _for_row_chunks(x1_ref.shape[0], norm_rows)

    h = h_ref[...]
    gate = jnp.dot(h, wg_ref[...], preferred_element_type=_F32)
    up = jnp.dot(h, wu_ref[...], preferred_element_type=_F32)
    act = (jax.nn.silu(gate) * up).astype(_BF16)

    @pl.when(j == 0)
    def _():
        o_ref[...] = jnp.dot(act, wo_ref[...], preferred_element_type=_F32)

    @pl.when(j > 0)
    def _():
        o_ref[...] += jnp.dot(act, wo_ref[...], preferred_element_type=_F32)

    @pl.when(j == pl.num_programs(1) - 1)
    def _():
        def rows_step(rows):
            f = o_ref[rows, :]
            o_ref[rows, :] = x1_ref[rows, :] + f * _rms_scale(f) * gpost_ref[...]
        _for_row_chunks(o_ref.shape[0], rows_step)


def _ffn(x1, gpre, w_in, w_out, gpost, *, bm):
    t, d = x1.shape
    _, nf, _, bf = w_in.shape
    return pl.pallas_call(
        _ffn_kernel,
        grid=(t // bm, nf),
        in_specs=[pl.BlockSpec((bm, d), lambda i, j: (i, 0), pipeline_mode=_SINGLE),
                  pl.BlockSpec((1, d), lambda i, j: (0, 0)),
                  pl.BlockSpec((None, None, d, bf), lambda i, j: (0, j, 0, 0)),
                  pl.BlockSpec((None, None, d, bf), lambda i, j: (1, j, 0, 0)),
                  pl.BlockSpec((bf, d), lambda i, j: (j, 0)),
                  pl.BlockSpec((1, d), lambda i, j: (0, 0))],
        out_specs=pl.BlockSpec((bm, d), lambda i, j: (i, 0), pipeline_mode=_SINGLE),
        out_shape=jax.ShapeDtypeStruct((t, d), _F32),
        scratch_shapes=[pltpu.VMEM((bm, d), _BF16)],
        compiler_params=_params(("arbitrary", "arbitrary")),
        name="swiglu_ffn",
    )(x1, gpre, w_in, w_in, w_out, gpost)


def kernel(x, pre_mix_g, w_in, gmlp_v_norm_g, w_spatial, b_spatial, w_conv, b_conv, w_r, b_r, w_i, b_i, lru_lambda, out_norm_a_g, out_norm_b_g, w_out, post_mix_g, pre_ffn_g, w_ffn_in, w_ffn_out, post_ffn_g):
    batch, seq, d = x.shape
    depth = w_in.shape[0]
    xt = x.reshape(batch * seq, d)
    for l in range(depth):
        proj = _norm_matmul(xt, pre_mix_g[l][None], w_in[l].astype(_BF16), bm=1024, bn=1024)
        yn = _mixers(proj, gmlp_v_norm_g[l][None], w_spatial[l], b_spatial[l].T,
                     w_conv[l], b_conv[l][None], w_r[l].astype(_BF16), b_r[l][None],
                     w_i[l].astype(_BF16), b_i[l][None], lru_lambda[l][None],
                     out_norm_a_g[l][None], out_norm_b_g[l][None],
                     batch=batch, seq=seq, tm=256)
        x1 = _out_proj(yn, w_out[l].astype(_BF16), xt, post_mix_g[l][None], bm=256, bn=1024)
        bf = 256
        d_ff = w_ffn_out.shape[1]
        w_gu = w_ffn_in[l].astype(_BF16).reshape(d, 2, d_ff // bf, bf).transpose(1, 2, 0, 3)
        xt = _ffn(x1, pre_ffn_g[l][None], w_gu, w_ffn_out[l].astype(_BF16),
                  post_ffn_g[l][None], bm=1024)
    return xt.reshape(batch, seq, d)
```

```python
import math
from functools import partial

import jax
import jax.numpy as jnp
from jax import lax
from jax.experimental import pallas as pl
from jax.experimental.pallas import tpu as pltpu

EPS = 1e-6
HEAD_DIM = 128
CHUNK = 128
CONV_WIDTH = 4
LRU_C = 8.0
SUBLANES = 8
ROW_CHUNK = 32
VMEM_LIMIT_BYTES = 60 * 1024 * 1024

_BF16 = jnp.bfloat16
_F32 = jnp.float32
_LOG2E = math.log2(math.e)
_GELU_A = -2.0 * math.sqrt(2.0 / math.pi) * _LOG2E
_GELU_B = _GELU_A * 0.044715


def _gelu(x):
    return x / (1.0 + jnp.exp2(x * (_GELU_A + _GELU_B * (x * x))))


def _rms_scale(x):
    return lax.rsqrt(jnp.mean(x * x, axis=-1, keepdims=True) + EPS)


def _for_row_chunks(n_rows, body):
    def step(c, carry):
        body(pl.ds(pl.multiple_of(c * ROW_CHUNK, ROW_CHUNK), ROW_CHUNK))
        return carry
    lax.fori_loop(0, n_rows // ROW_CHUNK, step, 0)


def _params(semantics):
    return pltpu.CompilerParams(dimension_semantics=semantics,
                                vmem_limit_bytes=VMEM_LIMIT_BYTES)


_SINGLE = pl.Buffered(1)


def _norm_matmul_kernel(x_ref, g_ref, w_ref, o_ref, hn_ref):
    @pl.when(pl.program_id(1) == 0)
    def _():
        def norm_rows(rows):
            x = x_ref[rows, :]
            hn_ref[rows, :] = (x * _rms_scale(x) * g_ref[...]).astype(_BF16)
        _for_row_chunks(x_ref.shape[0], norm_rows)

    o_ref[...] = jnp.dot(hn_ref[...], w_ref[...].astype(_BF16), preferred_element_type=_F32)


def _norm_matmul(x, g, w, *, bm, bn):
    t, d = x.shape
    n = w.shape[1]
    return pl.pallas_call(
        _norm_matmul_kernel,
        grid=(t // bm, n // bn),
        in_specs=[pl.BlockSpec((bm, d), lambda i, j: (i, 0), pipeline_mode=_SINGLE),
                  pl.BlockSpec((1, d), lambda i, j: (0, 0)),
                  pl.BlockSpec((d, bn), lambda i, j: (0, j))],
        out_specs=pl.BlockSpec((bm, bn), lambda i, j: (i, j)),
        out_shape=jax.ShapeDtypeStruct((t, n), _F32),
        scratch_shapes=[pltpu.VMEM((bm, d), _BF16)],
        compiler_params=_params(("arbitrary", "arbitrary")),
        name="norm_matmul_in",
    )(x, g, w)


def _scan_rows(a, b, carry):
    tm = a.shape[0]
    sub = lax.broadcasted_iota(jnp.int32, (SUBLANES, a.shape[1]), 0)
    hs = []
    for g in range(tm // SUBLANES):
        sl = slice(g * SUBLANES, (g + 1) * SUBLANES)
        ag, bg = a[sl], b[sl]
        d = 1
        while d < SUBLANES:
            keep = sub >= d
            a_sh = jnp.where(keep, pltpu.roll(ag, d, axis=0), 1.0)
            b_sh = jnp.where(keep, pltpu.roll(bg, d, axis=0), 0.0)
            bg = ag * b_sh + bg
            ag = ag * a_sh
            d *= 2
        h = ag * carry + bg
        hs.append(h)
        carry = jnp.broadcast_to(h[SUBLANES - 1:SUBLANES, :], h.shape)
    return jnp.concatenate(hs, axis=0), carry


def _mixer_kernel(u_ref, v_ref, gate_ref, xr_ref, vg_ref, ws_ref, bs_ref,
                  wc_ref, bc_ref, wr_ref, br_ref, wi_ref, bi_ref, lam_ref,
                  ga_ref, gb_ref, o_ref, ya_ref, yb_ref, xe_ref, carry_ref):
    tm, da = u_ref.shape
    db = xr_ref.shape[1]
    hist = SUBLANES

    @pl.when(pl.program_id(1) == 0)
    def _():
        xe_ref[0:hist, :] = jnp.zeros((hist, db), _F32)
        carry_ref[...] = jnp.zeros(carry_ref.shape, _F32)

    tril = (lax.broadcasted_iota(jnp.int32, (CHUNK, CHUNK), 0)
            >= lax.broadcasted_iota(jnp.int32, (CHUNK, CHUNK), 1))
    for h in range(da // HEAD_DIM):
        cols = slice(h * HEAD_DIM, (h + 1) * HEAD_DIM)
        gv = _gelu(v_ref[:, cols])
        vn = (gv * _rms_scale(gv) * vg_ref[:, cols]).astype(_BF16)
        ws = jnp.where(tril, ws_ref[h], 0.0).astype(_BF16)
        bias = jnp.broadcast_to(bs_ref[:, h:h + 1], (CHUNK, HEAD_DIM))
        for c in range(tm // CHUNK):
            rows = slice(c * CHUNK, (c + 1) * CHUNK)
            mixed = jnp.dot(ws, vn[rows], preferred_element_type=_F32) + bias
            ya_ref[rows, cols] = _gelu(u_ref[rows, cols]) * mixed

    xe_ref[hist:hist + tm, :] = xr_ref[...]
    neg = -lam_ref[...]
    softplus = jnp.maximum(neg, 0.0) + jnp.log1p(jnp.exp(-jnp.abs(neg)))
    log2_a_per_r = (-LRU_C * _LOG2E) * softplus
    for n in range(db // HEAD_DIM):
        cols = slice(n * HEAD_DIM, (n + 1) * HEAD_DIM)
        xc = bc_ref[:, cols]
        for k in range(CONV_WIDTH):
            tap = xe_ref[pl.ds(hist - (CONV_WIDTH - 1) + k, tm), cols]
            xc = xc + tap * wc_ref[k:k + 1, cols]
        xc16 = xc.astype(_BF16)
        r = jax.nn.sigmoid(jnp.dot(xc16, wr_ref[n], preferred_element_type=_F32) + br_ref[:, cols])
        i = jax.nn.sigmoid(jnp.dot(xc16, wi_ref[n], preferred_element_type=_F32) + bi_ref[:, cols])
        a = jnp.exp2(r * log2_a_per_r[:, cols])
        m2 = jnp.maximum(1.0 - a * a, 1e-12)
        bterm = (m2 * lax.rsqrt(m2)) * (i * xc)
        hseq, carry = _scan_rows(a, bterm, carry_ref[:, cols])
        carry_ref[:, cols] = carry
        yb_ref[:, cols] = hseq * _gelu(gate_ref[:, cols])
    xe_ref[0:hist, :] = xe_ref[tm:tm + hist, :]

    def norm_rows(rows):
        ya = ya_ref[rows, :]
        o_ref[rows, 0:da] = (ya * _rms_scale(ya) * ga_ref[...]).astype(o_ref.dtype)
        yb = yb_ref[rows, :]
        o_ref[rows, da:da + db] = (yb * _rms_scale(yb) * gb_ref[...]).astype(o_ref.dtype)
    _for_row_chunks(tm, norm_rows)


def _mixers(proj, vg, ws, bs_t, wc, bc, wr, br, wi, bi, lam, ga, gb, *, batch, seq, tm):
    t = proj.shape[0]
    da = vg.shape[1]
    db = lam.shape[1]
    ns = seq // tm
    nh = ws.shape[0]
    nb = wr.shape[0]

    def col_block(c):
        return pl.BlockSpec((tm, da), lambda b, s, c=c: (b * ns + s, c))

    def whole(shape):
        return pl.BlockSpec(shape, lambda b, s: (0,) * len(shape))

    return pl.pallas_call(
        _mixer_kernel,
        grid=(batch, ns),
        in_specs=[col_block(0), col_block(1), col_block(2), col_block(3),
                  whole((1, da)), whole((nh, CHUNK, CHUNK)), whole((CHUNK, nh)),
                  whole((CONV_WIDTH, db)), whole((1, db)),
                  whole((nb, HEAD_DIM, HEAD_DIM)), whole((1, db)),
                  whole((nb, HEAD_DIM, HEAD_DIM)), whole((1, db)), whole((1, db)),
                  whole((1, da)), whole((1, db))],
        out_specs=pl.BlockSpec((tm, da + db), lambda b, s: (b * ns + s, 0)),
        out_shape=jax.ShapeDtypeStruct((t, da + db), _BF16),
        scratch_shapes=[pltpu.VMEM((tm, da), _F32),
                        pltpu.VMEM((tm, db), _F32),
                        pltpu.VMEM((tm + 2 * SUBLANES, db), _F32),
                        pltpu.VMEM((SUBLANES, db), _F32)],
        compiler_params=_params(("arbitrary", "arbitrary")),
        name="token_mixers",
    )(proj, proj, proj, proj, vg, ws, bs_t, wc, bc, wr, br, wi, bi, lam, ga, gb)


def _out_proj_kernel(yn_ref, w_ref, x_ref, gpost_ref, gpre_ref, x1_ref, h2_ref, *, bn):
    yn = yn_ref[...]
    for n in range(w_ref.shape[1] // bn):
        cols = slice(n * bn, (n + 1) * bn)
        x1_ref[:, cols] = jnp.dot(yn, w_ref[:, cols], preferred_element_type=_F32)

    def rows_step(rows):
        y = x1_ref[rows, :]
        x1 = x_ref[rows, :] + y * _rms_scale(y) * gpost_ref[...]
        x1_ref[rows, :] = x1
        h2_ref[rows, :] = (x1 * _rms_scale(x1) * gpre_ref[...]).astype(h2_ref.dtype)
    _for_row_chunks(x1_ref.shape[0], rows_step)


def _out_proj(yn, w, x, gpost, gpre, *, bm, bn):
    t, k = yn.shape
    d = w.shape[1]
    row = lambda i: (i, 0)
    const = lambda i: (0, 0)
    return pl.pallas_call(
        partial(_out_proj_kernel, bn=bn),
        grid=(t // bm,),
        in_specs=[pl.BlockSpec((bm, k), row),
                  pl.BlockSpec((k, d), const),
                  pl.BlockSpec((bm, d), row),
                  pl.BlockSpec((1, d), const),
                  pl.BlockSpec((1, d), const)],
        out_specs=[pl.BlockSpec((bm, d), row), pl.BlockSpec((bm, d), row)],
        out_shape=[jax.ShapeDtypeStruct((t, d), _F32), jax.ShapeDtypeStruct((t, d), _BF16)],
        compiler_params=_params(("arbitrary",)),
        name="out_proj_resid",
    )(yn, w, x, gpost, gpre)


def _ffn_kernel(h_ref, x1_ref, wg_ref, wu_ref, wo_ref, gpost_ref, o_ref, *, nf):
    j = pl.program_id(1)

    @pl.when(j < nf)
    def _():
        h = h_ref[...]
        gate = jnp.dot(h, wg_ref[...].astype(_BF16), preferred_element_type=_F32)
        up = jnp.dot(h, wu_ref[...].astype(_BF16), preferred_element_type=_F32)
        act = (jax.nn.silu(gate) * up).astype(_BF16)

        @pl.when(j == 0)
        def _():
            o_ref[...] = jnp.dot(act, wo_ref[...], preferred_element_type=_F32)

        @pl.when(j > 0)
        def _():
            o_ref[...] += jnp.dot(act, wo_ref[...], preferred_element_type=_F32)

    @pl.when(j >= nf)
    def _():
        slice_rows = x1_ref.shape[0]
        base = (j - nf) * slice_rows

        def rows_step(rows):
            out_rows = pl.ds(pl.multiple_of(base + rows.start, ROW_CHUNK), ROW_CHUNK)
            f = o_ref[out_rows, :]
            o_ref[out_rows, :] = x1_ref[rows, :] + f * _rms_scale(f) * gpost_ref[...]
        _for_row_chunks(slice_rows, rows_step)


def _ffn(h2, x1, w_in, w_out, gpost, *, bm, bf, n_epi):
    t, d = h2.shape
    f = w_out.shape[0]
    nf = f // bf
    last = nf - 1
    return pl.pallas_call(
        partial(_ffn_kernel, nf=nf),
        grid=(t // bm, nf + n_epi),
        in_specs=[pl.BlockSpec((bm, d), lambda i, j: (i, 0), pipeline_mode=_SINGLE),
                  pl.BlockSpec((bm // n_epi, d),
                               lambda i, j: (i * n_epi + jnp.clip(j - nf, 0, n_epi - 1), 0)),
                  pl.BlockSpec((d, bf), lambda i, j: (0, jnp.minimum(j, last))),
                  pl.BlockSpec((d, bf), lambda i, j: (0, nf + jnp.minimum(j, last))),
                  pl.BlockSpec((bf, d), lambda i, j: (jnp.minimum(j, last), 0)),
                  pl.BlockSpec((1, d), lambda i, j: (0, 0))],
        out_specs=pl.BlockSpec((bm, d), lambda i, j: (i, 0), pipeline_mode=_SINGLE),
        out_shape=jax.ShapeDtypeStruct((t, d), _F32),
        compiler_params=_params(("arbitrary", "arbitrary")),
        name="swiglu_ffn",
    )(h2, x1, w_in, w_in, w_out, gpost)


def kernel(x, pre_mix_g, w_in, gmlp_v_norm_g, w_spatial, b_spatial, w_conv, b_conv, w_r, b_r, w_i, b_i, lru_lambda, out_norm_a_g, out_norm_b_g, w_out, post_mix_g, pre_ffn_g, w_ffn_in, w_ffn_out, post_ffn_g):
    batch, seq, d = x.shape
    depth = w_in.shape[0]
    xt = x.reshape(batch * seq, d)
    for l in range(depth):
        proj = _norm_matmul(xt, pre_mix_g[l][None], w_in[l], bm=1024, bn=512)
        yn = _mixers(proj, gmlp_v_norm_g[l][None], w_spatial[l], b_spatial[l].T,
                     w_conv[l], b_conv[l][None], w_r[l].astype(_BF16), b_r[l][None],
                     w_i[l].astype(_BF16), b_i[l][None], lru_lambda[l][None],
                     out_norm_a_g[l][None], out_norm_b_g[l][None],
                     batch=batch, seq=seq, tm=256)
        x1, h2 = _out_proj(yn, w_out[l].astype(_BF16), xt, post_mix_g[l][None], pre_ffn_g[l][None],
                           bm=256, bn=1024)
        xt = _ffn(h2, x1, w_ffn_in[l], w_ffn_out[l].astype(_BF16), post_ffn_g[l][None],
                  bm=1024, bf=256, n_epi=8)
    return xt.reshape(batch, seq, d)
```

```python
import math
from functools import partial

import jax
import jax.numpy as jnp
from jax import lax
from jax.experimental import pallas as pl
from jax.experimental.pallas import tpu as pltpu

EPS = 1e-6
HEAD_DIM = 128
CHUNK = 128
CONV_WIDTH = 4
LRU_C = 8.0
SUBLANES = 8
ROW_CHUNK = 32
SCAN_ROWS = 256
MAT_ROWS = 512
VMEM_LIMIT_BYTES = 60 * 1024 * 1024

_BF16 = jnp.bfloat16
_F32 = jnp.float32
_LOG2E = math.log2(math.e)
_GELU_A = -2.0 * math.sqrt(2.0 / math.pi) * _LOG2E
_GELU_B = _GELU_A * 0.044715


def _gelu(x):
    return x / (1.0 + jnp.exp2(x * (_GELU_A + _GELU_B * (x * x))))


def _rms_scale(x):
    return lax.rsqrt(jnp.mean(x * x, axis=-1, keepdims=True) + EPS)


def _for_row_chunks(n_rows, body):
    def step(c, carry):
        body(pl.ds(pl.multiple_of(c * ROW_CHUNK, ROW_CHUNK), ROW_CHUNK))
        return carry
    lax.fori_loop(0, n_rows // ROW_CHUNK, step, 0)


def _params(semantics):
    return pltpu.CompilerParams(dimension_semantics=semantics,
                                vmem_limit_bytes=VMEM_LIMIT_BYTES)


_SINGLE = pl.Buffered(1)


def _scan_rows(a, b, carry):
    tm = a.shape[0]
    sub = lax.broadcasted_iota(jnp.int32, (SUBLANES, a.shape[1]), 0)
    hs = []
    for g in range(tm // SUBLANES):
        sl = slice(g * SUBLANES, (g + 1) * SUBLANES)
        ag, bg = a[sl], b[sl]
        d = 1
        while d < SUBLANES:
            keep = sub >= d
            a_sh = jnp.where(keep, pltpu.roll(ag, d, axis=0), 1.0)
            b_sh = jnp.where(keep, pltpu.roll(bg, d, axis=0), 0.0)
            bg = ag * b_sh + bg
            ag = ag * a_sh
            d *= 2
        h = ag * carry + bg
        hs.append(h)
        carry = jnp.broadcast_to(h[SUBLANES - 1:SUBLANES, :], h.shape)
    return jnp.concatenate(hs, axis=0), carry


def _mix_tasks(p_ref, jp, seq_start, vg_ref, ws_ref, bs_ref, wc_ref, bc_ref, wr_ref, br_ref,
               wi_ref, bi_ref, lam_ref, ga_ref, gb_ref, ya_ref, yb_ref, qa_ref, qb_ref,
               xe_ref, hist_ref, carry_ref):
    _, bm, gw = p_ref.shape
    hist = SUBLANES
    n_sub = gw // HEAD_DIM
    gmlp_tasks, lru_tasks = [], []

    def gmlp_chunk(c):
        tril = (lax.broadcasted_iota(jnp.int32, (CHUNK, CHUNK), 0)
                >= lax.broadcasted_iota(jnp.int32, (CHUNK, CHUNK), 1))
        rows = slice(c * CHUNK, (c + 1) * CHUNK)
        sq = None
        for h in range(n_sub):
            cols = slice(h * HEAD_DIM, (h + 1) * HEAD_DIM)
            ws = jnp.where(tril, ws_ref[h], 0.0).astype(_BF16)
            gv = _gelu(p_ref[1, rows, cols])
            vn = (gv * _rms_scale(gv) * vg_ref[:, cols]).astype(_BF16)
            mixed = (jnp.dot(ws, vn, preferred_element_type=_F32)
                     + jnp.broadcast_to(bs_ref[h], (CHUNK, HEAD_DIM)))
            ya = _gelu(p_ref[0, rows, cols]) * mixed
            ya_ref[rows, cols] = (ya * ga_ref[:, cols]).astype(ya_ref.dtype)
            sq = ya * ya if sq is None else sq + ya * ya
        qa_ref[rows, :] = sq

    for c in range(bm // CHUNK):
        gmlp_tasks.append(partial(gmlp_chunk, c))

    def lru_setup():
        xe_ref[0:hist, :] = jnp.where(seq_start, jnp.zeros((hist, gw), _F32), hist_ref[jp])
        xe_ref[hist:hist + bm, :] = p_ref[3]

    def lru_subtile(n, r0):
        cols = slice(n * HEAD_DIM, (n + 1) * HEAD_DIM)
        rows = slice(r0, r0 + SCAN_ROWS)
        if r0 == 0:
            carry = jnp.where(seq_start, jnp.zeros((SUBLANES, HEAD_DIM), _F32), carry_ref[jp, :, cols])
        else:
            carry = carry_ref[jp, :, cols]
        neg = -lam_ref[:, cols]
        softplus = jnp.maximum(neg, 0.0) + jnp.log1p(jnp.exp(-jnp.abs(neg)))
        log2_a_per_r = (-LRU_C * _LOG2E) * softplus
        xc = bc_ref[:, cols]
        for k in range(CONV_WIDTH):
            tap = xe_ref[pl.ds(r0 + hist - (CONV_WIDTH - 1) + k, SCAN_ROWS), cols]
            xc = xc + tap * wc_ref[k:k + 1, cols]
        xc16 = xc.astype(_BF16)
        r = jax.nn.sigmoid(jnp.dot(xc16, wr_ref[n], preferred_element_type=_F32) + br_ref[:, cols])
        i = jax.nn.sigmoid(jnp.dot(xc16, wi_ref[n], preferred_element_type=_F32) + bi_ref[:, cols])
        a = jnp.exp2(r * log2_a_per_r)
        m2 = jnp.maximum(1.0 - a * a, 1e-12)
        bterm = (m2 * lax.rsqrt(m2)) * (i * xc)
        hseq, carry = _scan_rows(a, bterm, carry)
        carry_ref[jp, :, cols] = carry
        yb = hseq * _gelu(p_ref[2, rows, cols])
        yb_ref[rows, cols] = (yb * gb_ref[:, cols]).astype(yb_ref.dtype)
        if n == 0:
            qb_ref[rows, :] = yb * yb
        else:
            qb_ref[rows, :] += yb * yb

    def lru_finish():
        hist_ref[jp] = xe_ref[bm:bm + hist, :]

    for n in range(n_sub):
        for r0 in range(0, bm, SCAN_ROWS):
            lru_tasks.append(partial(lru_subtile, n, r0))
    return gmlp_tasks, lru_setup, lru_tasks, lru_finish


def _in_mix_kernel(x_ref, g_ref, wu_ref, wv_ref, wg_ref, wx_ref,
                   vg_ref, ws_ref, bs_ref, wc_ref, bc_ref, wr_ref, br_ref, wi_ref, bi_ref, lam_ref,
                   ga_ref, gb_ref, ya_ref, yb_ref, sa_ref, sb_ref,
                   hn_ref, pa_ref, pb_ref, qa_ref, qb_ref, xe_ref, hist_ref, carry_ref,
                   *, ng, blocks_per_seq):
    s = pl.program_id(0)
    n_mat = pl.num_programs(0) - 1
    j = lax.rem(jnp.minimum(s, n_mat - 1), ng)
    sp = jnp.maximum(s - 1, 0)
    ip, jp = sp // ng, lax.rem(sp, ng)
    seq_start = lax.rem(ip, blocks_per_seq) == 0

    @pl.when(s == 0)
    def _():
        pb_ref[...] = jnp.zeros(pb_ref.shape, _F32)
        hist_ref[...] = jnp.zeros(hist_ref.shape, _F32)
        carry_ref[...] = jnp.zeros(carry_ref.shape, _F32)

    @pl.when((j == 0) & (s < n_mat))
    def _():
        def norm_rows(rows):
            x = x_ref[rows, :]
            hn_ref[rows, :] = (x * _rms_scale(x) * g_ref[...]).astype(_BF16)
        _for_row_chunks(x_ref.shape[0], norm_rows)

    def step(mat_ref, mix_ref):
        bm = hn_ref.shape[0]
        mat_tasks = []
        for k, w_ref in enumerate((wu_ref, wv_ref, wg_ref, wx_ref)):
            for r0 in range(0, bm, MAT_ROWS):
                def mat_piece(k=k, w_ref=w_ref, r0=r0):
                    mat_ref[k, r0:r0 + MAT_ROWS, :] = jnp.dot(
                        hn_ref[r0:r0 + MAT_ROWS, :], w_ref[...], preferred_element_type=_F32)
                mat_tasks.append(mat_piece)
        gmlp_tasks, lru_setup, lru_tasks, lru_finish = _mix_tasks(
            mix_ref, jp, seq_start, vg_ref, ws_ref, bs_ref, wc_ref, bc_ref, wr_ref, br_ref,
            wi_ref, bi_ref, lam_ref, ga_ref, gb_ref, ya_ref, yb_ref, qa_ref, qb_ref,
            xe_ref, hist_ref, carry_ref)
        lru_setup()
        n = len(mat_tasks)
        for t in range(n):
            mat_tasks[t]()
            for task in gmlp_tasks[t * len(gmlp_tasks) // n:(t + 1) * len(gmlp_tasks) // n]:
                task()
            for task in lru_tasks[t * len(lru_tasks) // n:(t + 1) * len(lru_tasks) // n]:
                task()
        lru_finish()

    parity = lax.rem(s, 2)

    @pl.when(parity == 0)
    def _():
        step(pa_ref, pb_ref)

    @pl.when(parity == 1)
    def _():
        step(pb_ref, pa_ref)

    @pl.when(jp == 0)
    def _():
        sa_ref[...] = qa_ref[...]
        sb_ref[...] = qb_ref[...]

    @pl.when(jp > 0)
    def _():
        sa_ref[...] += qa_ref[...]
        sb_ref[...] += qb_ref[...]


def _in_mix(x, g, w, vg, ws, bs, wc, bc, wr, br, wi, bi, lam, ga, gb, *, seq, bm, gw):
    t, d = x.shape
    dh = vg.shape[1]
    ng = dh // gw
    n_sub = gw // HEAD_DIM
    n_mat = (t // bm) * ng

    def mat_idx(s):
        sm = jnp.minimum(s, n_mat - 1)
        return sm // ng, lax.rem(sm, ng)

    def mix_idx(s):
        sp = jnp.maximum(s - 1, 0)
        return sp // ng, lax.rem(sp, ng)

    def w_spec(c):
        return pl.BlockSpec((d, gw), lambda s, c=c: (0, c * ng + mat_idx(s)[1]))

    def vec(rows=1):
        return pl.BlockSpec((rows, gw), lambda s: (0, mix_idx(s)[1]))

    def per_head(shape):
        return pl.BlockSpec((n_sub,) + shape, lambda s: (mix_idx(s)[1], 0, 0))

    y_spec = pl.BlockSpec((bm, gw), lambda s: mix_idx(s))
    q_spec = pl.BlockSpec((bm, HEAD_DIM), lambda s: (mix_idx(s)[0], 0))
    return pl.pallas_call(
        partial(_in_mix_kernel, ng=ng, blocks_per_seq=seq // bm),
        grid=(n_mat + 1,),
        in_specs=[pl.BlockSpec((bm, d), lambda s: (mat_idx(s)[0], 0), pipeline_mode=_SINGLE),
                  pl.BlockSpec((1, d), lambda s: (0, 0)),
                  w_spec(0), w_spec(1), w_spec(2), w_spec(3),
                  vec(), per_head((CHUNK, CHUNK)), per_head((CHUNK, 1)),
                  vec(CONV_WIDTH), vec(),
                  per_head((HEAD_DIM, HEAD_DIM)), vec(),
                  per_head((HEAD_DIM, HEAD_DIM)), vec(), vec(),
                  vec(), vec()],
        out_specs=[y_spec, y_spec, q_spec, q_spec],
        out_shape=[jax.ShapeDtypeStruct((t, dh), _BF16), jax.ShapeDtypeStruct((t, dh), _BF16),
                   jax.ShapeDtypeStruct((t, HEAD_DIM), _F32), jax.ShapeDtypeStruct((t, HEAD_DIM), _F32)],
        scratch_shapes=[pltpu.VMEM((bm, d), _BF16),
                        pltpu.VMEM((4, bm, gw), _F32),
                        pltpu.VMEM((4, bm, gw), _F32),
                        pltpu.VMEM((bm, HEAD_DIM), _F32),
                        pltpu.VMEM((bm, HEAD_DIM), _F32),
                        pltpu.VMEM((bm + 2 * SUBLANES, gw), _F32),
                        pltpu.VMEM((ng, SUBLANES, gw), _F32),
                        pltpu.VMEM((ng, SUBLANES, gw), _F32)],
        compiler_params=_params(("arbitrary",)),
        name="in_proj_mixers",
    )(x, g, w, w, w, w, vg, ws, bs, wc, bc, wr, br, wi, bi, lam, ga, gb)


def _out_proj_kernel(ya_ref, yb_ref, qa_ref, qb_ref, w_ref, x_ref, gpost_ref, gpre_ref,
                     x1_ref, h2_ref, *, bn):
    da = ya_ref.shape[1]
    db = yb_ref.shape[1]
    scale_a = lax.rsqrt(jnp.sum(qa_ref[...], axis=-1, keepdims=True) * (1.0 / da) + EPS)
    scale_b = lax.rsqrt(jnp.sum(qb_ref[...], axis=-1, keepdims=True) * (1.0 / db) + EPS)
    ya = ya_ref[...]
    yb = yb_ref[...]
    for n in range(w_ref.shape[1] // bn):
        cols = slice(n * bn, (n + 1) * bn)
        x1_ref[:, cols] = (
            scale_a * jnp.dot(ya, w_ref[0:da, cols], preferred_element_type=_F32)
            + scale_b * jnp.dot(yb, w_ref[da:da + db, cols], preferred_element_type=_F32))

    def rows_step(rows):
        y = x1_ref[rows, :]
        x1 = x_ref[rows, :] + y * _rms_scale(y) * gpost_ref[...]
        x1_ref[rows, :] = x1
        h2_ref[rows, :] = (x1 * _rms_scale(x1) * gpre_ref[...]).astype(h2_ref.dtype)
    _for_row_chunks(x1_ref.shape[0], rows_step)


def _out_proj(ya, yb, qa, qb, w, x, gpost, gpre, *, bm, bn):
    t, da = ya.shape
    db = yb.shape[1]
    d = w.shape[1]
    row = lambda i: (i, 0)
    const = lambda i: (0, 0)
    return pl.pallas_call(
        partial(_out_proj_kernel, bn=bn),
        grid=(t // bm,),
        in_specs=[pl.BlockSpec((bm, da), row),
                  pl.BlockSpec((bm, db), row),
                  pl.BlockSpec((bm, HEAD_DIM), row),
                  pl.BlockSpec((bm, HEAD_DIM), row),
                  pl.BlockSpec((da + db, d), const),
                  pl.BlockSpec((bm, d), row),
                  pl.BlockSpec((1, d), const),
                  pl.BlockSpec((1, d), const)],
        out_specs=[pl.BlockSpec((bm, d), row), pl.BlockSpec((bm, d), row)],
        out_shape=[jax.ShapeDtypeStruct((t, d), _F32), jax.ShapeDtypeStruct((t, d), _BF16)],
        compiler_params=_params(("arbitrary",)),
        name="out_proj_resid",
    )(ya, yb, qa, qb, w, x, gpost, gpre)


def _ffn_kernel(h_ref, x1_ref, wg_ref, wu_ref, wo_ref, gpost_ref, o_ref, *, nf):
    j = pl.program_id(1)

    @pl.when(j < nf)
    def _():
        h = h_ref[...]
        gate = jnp.dot(h, wg_ref[...].astype(_BF16), preferred_element_type=_F32)
        up = jnp.dot(h, wu_ref[...].astype(_BF16), preferred_element_type=_F32)
        act = (jax.nn.silu(gate) * up).astype(_BF16)

        @pl.when(j == 0)
        def _():
            o_ref[...] = jnp.dot(act, wo_ref[...], preferred_element_type=_F32)

        @pl.when(j > 0)
        def _():
            o_ref[...] += jnp.dot(act, wo_ref[...], preferred_element_type=_F32)

    @pl.when(j >= nf)
    def _():
        slice_rows = x1_ref.shape[0]
        base = (j - nf) * slice_rows

        def rows_step(rows):
            out_rows = pl.ds(pl.multiple_of(base + rows.start, ROW_CHUNK), ROW_CHUNK)
            f = o_ref[out_rows, :]
            o_ref[out_rows, :] = x1_ref[rows, :] + f * _rms_scale(f) * gpost_ref[...]
        _for_row_chunks(slice_rows, rows_step)


def _ffn(h2, x1, w_in, w_out, gpost, *, bm, bf, n_epi):
    t, d = h2.shape
    f = w_out.shape[0]
    nf = f // bf
    last = nf - 1
    return pl.pallas_call(
        partial(_ffn_kernel, nf=nf),
        grid=(t // bm, nf + n_epi),
        in_specs=[pl.BlockSpec((bm, d), lambda i, j: (i, 0), pipeline_mode=_SINGLE),
                  pl.BlockSpec((bm // n_epi, d),
                               lambda i, j: (i * n_epi + jnp.clip(j - nf, 0, n_epi - 1), 0)),
                  pl.BlockSpec((d, bf), lambda i, j: (0, jnp.minimum(j, last))),
                  pl.BlockSpec((d, bf), lambda i, j: (0, nf + jnp.minimum(j, last))),
                  pl.BlockSpec((bf, d), lambda i, j: (jnp.minimum(j, last), 0)),
                  pl.BlockSpec((1, d), lambda i, j: (0, 0))],
        out_specs=pl.BlockSpec((bm, d), lambda i, j: (i, 0), pipeline_mode=_SINGLE),
        out_shape=jax.ShapeDtypeStruct((t, d), _F32),
        compiler_params=_params(("arbitrary", "arbitrary")),
        name="swiglu_ffn",
    )(h2, x1, w_in, w_in, w_out, gpost)


def kernel(x, pre_mix_g, w_in, gmlp_v_norm_g, w_spatial, b_spatial, w_conv, b_conv, w_r, b_r, w_i, b_i, lru_lambda, out_norm_a_g, out_norm_b_g, w_out, post_mix_g, pre_ffn_g, w_ffn_in, w_ffn_out, post_ffn_g):
    batch, seq, d = x.shape
    depth = w_in.shape[0]
    xt = x.reshape(batch * seq, d)
    for l in range(depth):
        ya, yb, qa, qb = _in_mix(
            xt, pre_mix_g[l][None], w_in[l].astype(_BF16), gmlp_v_norm_g[l][None], w_spatial[l],
            b_spatial[l][:, :, None], w_conv[l], b_conv[l][None], w_r[l].astype(_BF16), b_r[l][None],
            w_i[l].astype(_BF16), b_i[l][None], lru_lambda[l][None],
            out_norm_a_g[l][None], out_norm_b_g[l][None], seq=seq, bm=1024, gw=256)
        x1, h2 = _out_proj(ya, yb, qa, qb, w_out[l].astype(_BF16), xt, post_mix_g[l][None],
                           pre_ffn_g[l][None], bm=256, bn=512)
        xt = _ffn(h2, x1, w_ffn_in[l], w_ffn_out[l].astype(_BF16), post_ffn_g[l][None],
                  bm=1024, bf=256, n_epi=8)
    return xt.reshape(batch, seq, d)
```

```python
import math
from functools import partial

import jax
import jax.numpy as jnp
from jax import lax
from jax.experimental import pallas as pl
from jax.experimental.pallas import tpu as pltpu

EPS = 1e-6
HEAD_DIM = 128
CHUNK = 128
CONV_WIDTH = 4
LRU_C = 8.0
SUBLANES = 8
ROW_CHUNK = 32
SCAN_ROWS = 128
MAT_ROWS = 256
VMEM_LIMIT_BYTES = 60 * 1024 * 1024

_BF16 = jnp.bfloat16
_F32 = jnp.float32
_LOG2E = math.log2(math.e)
_GELU_A = -2.0 * math.sqrt(2.0 / math.pi) * _LOG2E
_GELU_B = _GELU_A * 0.044715


def _gelu(x):
    return x / (1.0 + jnp.exp2(x * (_GELU_A + _GELU_B * (x * x))))


def _rms_scale(x):
    return lax.rsqrt(jnp.mean(x * x, axis=-1, keepdims=True) + EPS)


def _for_row_chunks(n_rows, body):
    def step(c, carry):
        body(pl.ds(pl.multiple_of(c * ROW_CHUNK, ROW_CHUNK), ROW_CHUNK))
        return carry
    lax.fori_loop(0, n_rows // ROW_CHUNK, step, 0)


def _params(semantics):
    return pltpu.CompilerParams(dimension_semantics=semantics,
                                vmem_limit_bytes=VMEM_LIMIT_BYTES)


_SINGLE = pl.Buffered(1)


def _scan_rows(a, b, carry):
    tm = a.shape[0]
    sub = lax.broadcasted_iota(jnp.int32, (SUBLANES, a.shape[1]), 0)
    hs = []
    for g in range(tm // SUBLANES):
        sl = slice(g * SUBLANES, (g + 1) * SUBLANES)
        ag, bg = a[sl], b[sl]
        d = 1
        while d < SUBLANES:
            keep = sub >= d
            a_sh = jnp.where(keep, pltpu.roll(ag, d, axis=0), 1.0)
            b_sh = jnp.where(keep, pltpu.roll(bg, d, axis=0), 0.0)
            bg = ag * b_sh + bg
            ag = ag * a_sh
            d *= 2
        h = ag * carry + bg
        hs.append(h)
        carry = jnp.broadcast_to(h[SUBLANES - 1:SUBLANES, :], h.shape)
    return jnp.concatenate(hs, axis=0), carry


def _mix_tasks(p_ref, jp, seq_start, vg_ref, ws_ref, bs_ref, wc_ref, bc_ref, wr_ref, br_ref,
               wi_ref, bi_ref, lam_ref, ga_ref, gb_ref, ya_ref, yb_ref, qa_ref, qb_ref,
               xe_ref, hist_ref, carry_ref):
    _, bm, gw = p_ref.shape
    hist = SUBLANES
    n_sub = gw // HEAD_DIM
    gmlp_tasks, lru_tasks = [], []

    def gmlp_chunk(c):
        tril = (lax.broadcasted_iota(jnp.int32, (CHUNK, CHUNK), 0)
                >= lax.broadcasted_iota(jnp.int32, (CHUNK, CHUNK), 1))
        rows = slice(c * CHUNK, (c + 1) * CHUNK)
        sq = None
        for h in range(n_sub):
            cols = slice(h * HEAD_DIM, (h + 1) * HEAD_DIM)
            ws = jnp.where(tril, ws_ref[h], 0.0).astype(_BF16)
            gv = _gelu(p_ref[1, rows, cols])
            vn = (gv * _rms_scale(gv) * vg_ref[:, cols]).astype(_BF16)
            mixed = (jnp.dot(ws, vn, preferred_element_type=_F32)
                     + jnp.broadcast_to(bs_ref[h], (CHUNK, HEAD_DIM)))
            ya = _gelu(p_ref[0, rows, cols]) * mixed
            ya_ref[rows, cols] = (ya * ga_ref[:, cols]).astype(ya_ref.dtype)
            sq = ya * ya if sq is None else sq + ya * ya
        qa_ref[rows, :] = sq

    for c in range(bm // CHUNK):
        gmlp_tasks.append(partial(gmlp_chunk, c))

    def lru_setup():
        xe_ref[0:hist, :] = jnp.where(seq_start, jnp.zeros((hist, gw), _F32), hist_ref[jp])
        xe_ref[hist:hist + bm, :] = p_ref[3]

    def lru_subtile(n, r0):
        cols = slice(n * HEAD_DIM, (n + 1) * HEAD_DIM)
        rows = slice(r0, r0 + SCAN_ROWS)
        if r0 == 0:
            carry = jnp.where(seq_start, jnp.zeros((SUBLANES, HEAD_DIM), _F32), carry_ref[jp, :, cols])
        else:
            carry = carry_ref[jp, :, cols]
        neg = -lam_ref[:, cols]
        softplus = jnp.maximum(neg, 0.0) + jnp.log1p(jnp.exp(-jnp.abs(neg)))
        log2_a_per_r = (-LRU_C * _LOG2E) * softplus
        xc = bc_ref[:, cols]
        for k in range(CONV_WIDTH):
            tap = xe_ref[pl.ds(r0 + hist - (CONV_WIDTH - 1) + k, SCAN_ROWS), cols]
            xc = xc + tap * wc_ref[k:k + 1, cols]
        xc16 = xc.astype(_BF16)
        w_ri = jnp.concatenate([wr_ref[n], wi_ref[n]], axis=-1)
        ri = jnp.dot(xc16, w_ri, preferred_element_type=_F32)
        r = jax.nn.sigmoid(ri[:, :HEAD_DIM] + br_ref[:, cols])
        i = jax.nn.sigmoid(ri[:, HEAD_DIM:] + bi_ref[:, cols])
        a = jnp.exp2(r * log2_a_per_r)
        m2 = jnp.maximum(1.0 - a * a, 1e-12)
        bterm = (m2 * lax.rsqrt(m2)) * (i * xc)
        hseq, carry = _scan_rows(a, bterm, carry)
        carry_ref[jp, :, cols] = carry
        yb = hseq * _gelu(p_ref[2, rows, cols])
        yb_ref[rows, cols] = (yb * gb_ref[:, cols]).astype(yb_ref.dtype)
        if n == 0:
            qb_ref[rows, :] = yb * yb
        else:
            qb_ref[rows, :] += yb * yb

    def lru_finish():
        hist_ref[jp] = xe_ref[bm:bm + hist, :]

    for n in range(n_sub):
        for r0 in range(0, bm, SCAN_ROWS):
            lru_tasks.append(partial(lru_subtile, n, r0))
    return gmlp_tasks, lru_setup, lru_tasks, lru_finish


def _in_mix_kernel(x_ref, g_ref, wu_ref, wv_ref, wg_ref, wx_ref,
                   vg_ref, ws_ref, bs_ref, wc_ref, bc_ref, wr_ref, br_ref, wi_ref, bi_ref, lam_ref,
                   ga_ref, gb_ref, ya_ref, yb_ref, sa_ref, sb_ref,
                   hn_ref, pa_ref, pb_ref, qa_ref, qb_ref, xe_ref, hist_ref, carry_ref,
                   *, ng, blocks_per_seq):
    s = pl.program_id(0)
    n_mat = pl.num_programs(0) - 1
    j = lax.rem(jnp.minimum(s, n_mat - 1), ng)
    sp = jnp.maximum(s - 1, 0)
    ip, jp = sp // ng, lax.rem(sp, ng)
    seq_start = lax.rem(ip, blocks_per_seq) == 0

    @pl.when(s == 0)
    def _():
        pb_ref[...] = jnp.zeros(pb_ref.shape, _F32)
        hist_ref[...] = jnp.zeros(hist_ref.shape, _F32)
        carry_ref[...] = jnp.zeros(carry_ref.shape, _F32)

    @pl.when((j == 0) & (s < n_mat))
    def _():
        def norm_rows(rows):
            x = x_ref[rows, :]
            hn_ref[rows, :] = (x * _rms_scale(x) * g_ref[...]).astype(_BF16)
        _for_row_chunks(x_ref.shape[0], norm_rows)

    def step(mat_ref, mix_ref):
        bm = hn_ref.shape[0]
        mat_tasks = []
        for k, w_ref in enumerate((wu_ref, wv_ref, wg_ref, wx_ref)):
            for r0 in range(0, bm, MAT_ROWS):
                def mat_piece(k=k, w_ref=w_ref, r0=r0):
                    mat_ref[k, r0:r0 + MAT_ROWS, :] = jnp.dot(
                        hn_ref[r0:r0 + MAT_ROWS, :], w_ref[...], preferred_element_type=_F32)
                mat_tasks.append(mat_piece)
        gmlp_tasks, lru_setup, lru_tasks, lru_finish = _mix_tasks(
            mix_ref, jp, seq_start, vg_ref, ws_ref, bs_ref, wc_ref, bc_ref, wr_ref, br_ref,
            wi_ref, bi_ref, lam_ref, ga_ref, gb_ref, ya_ref, yb_ref, qa_ref, qb_ref,
            xe_ref, hist_ref, carry_ref)
        lru_setup()
        n = len(mat_tasks)
        for t in range(n):
            mat_tasks[t]()
            for task in gmlp_tasks[t * len(gmlp_tasks) // n:(t + 1) * len(gmlp_tasks) // n]:
                task()
            for task in lru_tasks[t * len(lru_tasks) // n:(t + 1) * len(lru_tasks) // n]:
                task()
        lru_finish()

    parity = lax.rem(s, 2)

    @pl.when(parity == 0)
    def _():
        step(pa_ref, pb_ref)

    @pl.when(parity == 1)
    def _():
        step(pb_ref, pa_ref)

    @pl.when(jp == 0)
    def _():
        sa_ref[...] = qa_ref[...]
        sb_ref[...] = qb_ref[...]

    @pl.when(jp > 0)
    def _():
        sa_ref[...] += qa_ref[...]
        sb_ref[...] += qb_ref[...]


def _in_mix(x, g, w, vg, ws, bs, wc, bc, wr, br, wi, bi, lam, ga, gb, *, seq, bm, gw):
    t, d = x.shape
    dh = vg.shape[1]
    ng = dh // gw
    n_sub = gw // HEAD_DIM
    n_mat = (t // bm) * ng

    def mat_idx(s):
        sm = jnp.minimum(s, n_mat - 1)
        return sm // ng, lax.rem(sm, ng)

    def mix_idx(s):
        sp = jnp.maximum(s - 1, 0)
        return sp // ng, lax.rem(sp, ng)

    def w_spec(c):
        return pl.BlockSpec((d, gw), lambda s, c=c: (0, c * ng + mat_idx(s)[1]))

    def vec(rows=1):
        return pl.BlockSpec((rows, gw), lambda s: (0, mix_idx(s)[1]))

    def per_head(shape):
        return pl.BlockSpec((n_sub,) + shape, lambda s: (mix_idx(s)[1], 0, 0))

    y_spec = pl.BlockSpec((bm, gw), lambda s: mix_idx(s))
    q_spec = pl.BlockSpec((bm, HEAD_DIM), lambda s: (mix_idx(s)[0], 0))
    return pl.pallas_call(
        partial(_in_mix_kernel, ng=ng, blocks_per_seq=seq // bm),
        grid=(n_mat + 1,),
        in_specs=[pl.BlockSpec((bm, d), lambda s: (mat_idx(s)[0], 0), pipeline_mode=_SINGLE),
                  pl.BlockSpec((1, d), lambda s: (0, 0)),
                  w_spec(0), w_spec(1), w_spec(2), w_spec(3),
                  vec(), per_head((CHUNK, CHUNK)), per_head((CHUNK, 1)),
                  vec(CONV_WIDTH), vec(),
                  per_head((HEAD_DIM, HEAD_DIM)), vec(),
                  per_head((HEAD_DIM, HEAD_DIM)), vec(), vec(),
                  vec(), vec()],
        out_specs=[y_spec, y_spec, q_spec, q_spec],
        out_shape=[jax.ShapeDtypeStruct((t, dh), _BF16), jax.ShapeDtypeStruct((t, dh), _BF16),
                   jax.ShapeDtypeStruct((t, HEAD_DIM), _F32), jax.ShapeDtypeStruct((t, HEAD_DIM), _F32)],
        scratch_shapes=[pltpu.VMEM((bm, d), _BF16),
                        pltpu.VMEM((4, bm, gw), _F32),
                        pltpu.VMEM((4, bm, gw), _F32),
                        pltpu.VMEM((bm, HEAD_DIM), _F32),
                        pltpu.VMEM((bm, HEAD_DIM), _F32),
                        pltpu.VMEM((bm + 2 * SUBLANES, gw), _F32),
                        pltpu.VMEM((ng, SUBLANES, gw), _F32),
                        pltpu.VMEM((ng, SUBLANES, gw), _F32)],
        compiler_params=_params(("arbitrary",)),
        name="in_proj_mixers",
    )(x, g, w, w, w, w, vg, ws, bs, wc, bc, wr, br, wi, bi, lam, ga, gb)


def _out_proj_kernel(ya_ref, yb_ref, qa_ref, qb_ref, w_ref, x_ref, gpost_ref, gpre_ref,
                     x1_ref, h2_ref, *, bn):
    da = ya_ref.shape[1]
    db = yb_ref.shape[1]
    scale_a = lax.rsqrt(jnp.sum(qa_ref[...], axis=-1, keepdims=True) * (1.0 / da) + EPS)
    scale_b = lax.rsqrt(jnp.sum(qb_ref[...], axis=-1, keepdims=True) * (1.0 / db) + EPS)
    ya = ya_ref[...]
    yb = yb_ref[...]
    for n in range(w_ref.shape[1] // bn):
        cols = slice(n * bn, (n + 1) * bn)
        x1_ref[:, cols] = (
            scale_a * jnp.dot(ya, w_ref[0:da, cols], preferred_element_type=_F32)
            + scale_b * jnp.dot(yb, w_ref[da:da + db, cols], preferred_element_type=_F32))

    def rows_step(rows):
        y = x1_ref[rows, :]
        x1 = x_ref[rows, :] + y * _rms_scale(y) * gpost_ref[...]
        x1_ref[rows, :] = x1
        h2_ref[rows, :] = (x1 * _rms_scale(x1) * gpre_ref[...]).astype(h2_ref.dtype)
    _for_row_chunks(x1_ref.shape[0], rows_step)


def _out_proj(ya, yb, qa, qb, w, x, gpost, gpre, *, bm, bn):
    t, da = ya.shape
    db = yb.shape[1]
    d = w.shape[1]
    row = lambda i: (i, 0)
    const = lambda i: (0, 0)
    return pl.pallas_call(
        partial(_out_proj_kernel, bn=bn),
        grid=(t // bm,),
        in_specs=[pl.BlockSpec((bm, da), row),
                  pl.BlockSpec((bm, db), row),
                  pl.BlockSpec((bm, HEAD_DIM), row),
                  pl.BlockSpec((bm, HEAD_DIM), row),
                  pl.BlockSpec((da + db, d), const),
                  pl.BlockSpec((bm, d), row),
                  pl.BlockSpec((1, d), const),
                  pl.BlockSpec((1, d), const)],
        out_specs=[pl.BlockSpec((bm, d), row), pl.BlockSpec((bm, d), row)],
        out_shape=[jax.ShapeDtypeStruct((t, d), _F32), jax.ShapeDtypeStruct((t, d), _BF16)],
        compiler_params=_params(("arbitrary",)),
        name="out_proj_resid",
    )(ya, yb, qa, qb, w, x, gpost, gpre)


def _ffn_kernel(h_ref, x1_ref, wg_ref, wu_ref, wo_ref, gpost_ref, o_ref, *, nf):
    j = pl.program_id(1)

    @pl.when(j < nf)
    def _():
        h = h_ref[...]
        gate = jnp.dot(h, wg_ref[...].astype(_BF16), preferred_element_type=_F32)
        up = jnp.dot(h, wu_ref[...].astype(_BF16), preferred_element_type=_F32)
        act = (jax.nn.silu(gate) * up).astype(_BF16)

        @pl.when(j == 0)
        def _():
            o_ref[...] = jnp.dot(act, wo_ref[...].astype(_BF16), preferred_element_type=_F32)

        @pl.when(j > 0)
        def _():
            o_ref[...] += jnp.dot(act, wo_ref[...].astype(_BF16), preferred_element_type=_F32)

    @pl.when(j >= nf)
    def _():
        slice_rows = x1_ref.shape[0]
        base = (j - nf) * slice_rows

        def rows_step(rows):
            out_rows = pl.ds(pl.multiple_of(base + rows.start, ROW_CHUNK), ROW_CHUNK)
            f = o_ref[out_rows, :]
            o_ref[out_rows, :] = x1_ref[rows, :] + f * _rms_scale(f) * gpost_ref[...]
        _for_row_chunks(slice_rows, rows_step)


def _ffn(h2, x1, w_in, w_out, gpost, *, bm, bf, n_epi):
    t, d = h2.shape
    f = w_out.shape[0]
    nf = f // bf
    last = nf - 1
    return pl.pallas_call(
        partial(_ffn_kernel, nf=nf),
        grid=(t // bm, nf + n_epi),
        in_specs=[pl.BlockSpec((bm, d), lambda i, j: (i, 0), pipeline_mode=_SINGLE),
                  pl.BlockSpec((bm // n_epi, d),
                               lambda i, j: (i * n_epi + jnp.clip(j - nf, 0, n_epi - 1), 0)),
                  pl.BlockSpec((d, bf), lambda i, j: (0, jnp.minimum(j, last))),
                  pl.BlockSpec((d, bf), lambda i, j: (0, nf + jnp.minimum(j, last))),
                  pl.BlockSpec((bf, d), lambda i, j: (jnp.minimum(j, last), 0)),
                  pl.BlockSpec((1, d), lambda i, j: (0, 0))],
        out_specs=pl.BlockSpec((bm, d), lambda i, j: (i, 0), pipeline_mode=_SINGLE),
        out_shape=jax.ShapeDtypeStruct((t, d), _F32),
        compiler_params=_params(("arbitrary", "arbitrary")),
        name="swiglu_ffn",
    )(h2, x1, w_in, w_in, w_out, gpost)


def kernel(x, pre_mix_g, w_in, gmlp_v_norm_g, w_spatial, b_spatial, w_conv, b_conv, w_r, b_r, w_i, b_i, lru_lambda, out_norm_a_g, out_norm_b_g, w_out, post_mix_g, pre_ffn_g, w_ffn_in, w_ffn_out, post_ffn_g):
    batch, seq, d = x.shape
    depth = w_in.shape[0]
    xt = x.reshape(batch * seq, d)
    for l in range(depth):
        ya, yb, qa, qb = _in_mix(
            xt, pre_mix_g[l][None], w_in[l].astype(_BF16), gmlp_v_norm_g[l][None], w_spatial[l],
            b_spatial[l][:, :, None], w_conv[l], b_conv[l][None], w_r[l].astype(_BF16), b_r[l][None],
            w_i[l].astype(_BF16), b_i[l][None], lru_lambda[l][None],
            out_norm_a_g[l][None], out_norm_b_g[l][None], seq=seq, bm=1024, gw=256)
        x1, h2 = _out_proj(ya, yb, qa, qb, w_out[l].astype(_BF16), xt, post_mix_g[l][None],
                           pre_ffn_g[l][None], bm=256, bn=512)
        xt = _ffn(h2, x1, w_ffn_in[l], w_ffn_out[l], post_ffn_g[l][None],
                  bm=1024, bf=256, n_epi=16)
    return xt.reshape(batch, seq, d)
```

```python
import math
from functools import partial

import jax
import jax.numpy as jnp
from jax import lax
from jax.experimental import pallas as pl
from jax.experimental.pallas import tpu as pltpu

EPS = 1e-6
HEAD_DIM = 128
CHUNK = 128
CONV_WIDTH = 4
LRU_C = 8.0
SUBLANES = 8
ROW_CHUNK = 32
SCAN_ROWS = 128
MAT_ROWS = 256
FFN_ROWS = 256
VMEM_LIMIT_BYTES = 60 * 1024 * 1024

_BF16 = jnp.bfloat16
_F32 = jnp.float32
_LOG2E = math.log2(math.e)
_GELU_A = -2.0 * math.sqrt(2.0 / math.pi) * _LOG2E
_GELU_B = _GELU_A * 0.044715


def _gelu(x):
    return x / (1.0 + jnp.exp2(x * (_GELU_A + _GELU_B * (x * x))))


def _rms_scale(x):
    return lax.rsqrt(jnp.mean(x * x, axis=-1, keepdims=True) + EPS)


def _for_row_chunks(n_rows, body):
    def step(c, carry):
        body(pl.ds(pl.multiple_of(c * ROW_CHUNK, ROW_CHUNK), ROW_CHUNK))
        return carry
    lax.fori_loop(0, n_rows // ROW_CHUNK, step, 0, unroll=2)


def _params(semantics):
    return pltpu.CompilerParams(dimension_semantics=semantics,
                                vmem_limit_bytes=VMEM_LIMIT_BYTES)


_SINGLE = pl.Buffered(1)


def _scan_rows(a, b, carry):
    tm = a.shape[0]
    sub = lax.broadcasted_iota(jnp.int32, (SUBLANES, a.shape[1]), 0)
    hs = []
    for g in range(tm // SUBLANES):
        sl = slice(g * SUBLANES, (g + 1) * SUBLANES)
        ag, bg = a[sl], b[sl]
        d = 1
        while d < SUBLANES:
            keep = sub >= d
            a_sh = jnp.where(keep, pltpu.roll(ag, d, axis=0), 1.0)
            b_sh = jnp.where(keep, pltpu.roll(bg, d, axis=0), 0.0)
            bg = ag * b_sh + bg
            ag = ag * a_sh
            d *= 2
        h = ag * carry + bg
        hs.append(h)
        carry = jnp.broadcast_to(h[SUBLANES - 1:SUBLANES, :], h.shape)
    return jnp.concatenate(hs, axis=0), carry


def _mix_tasks(p_ref, jp, seq_start, vg_ref, ws_ref, bs_ref, wc_ref, bc_ref, wr_ref, br_ref,
               wi_ref, bi_ref, lam_ref, ga_ref, gb_ref, ya_ref, yb_ref, qa_ref, qb_ref,
               xe_ref, hist_ref, carry_ref):
    _, bm, gw = p_ref.shape
    hist = SUBLANES
    n_sub = gw // HEAD_DIM
    gmlp_tasks, lru_tasks = [], []

    def gmlp_chunk(c):
        tril = (lax.broadcasted_iota(jnp.int32, (CHUNK, CHUNK), 0)
                >= lax.broadcasted_iota(jnp.int32, (CHUNK, CHUNK), 1))
        rows = slice(c * CHUNK, (c + 1) * CHUNK)
        sq = None
        for h in range(n_sub):
            cols = slice(h * HEAD_DIM, (h + 1) * HEAD_DIM)
            ws = jnp.where(tril, ws_ref[h], 0.0).astype(_BF16)
            gv = _gelu(p_ref[1, rows, cols])
            vn = (gv * _rms_scale(gv) * vg_ref[:, cols]).astype(_BF16)
            mixed = (jnp.dot(ws, vn, preferred_element_type=_F32)
                     + jnp.broadcast_to(bs_ref[h], (CHUNK, HEAD_DIM)))
            ya = _gelu(p_ref[0, rows, cols]) * mixed
            ya_ref[rows, cols] = (ya * ga_ref[:, cols]).astype(ya_ref.dtype)
            sq = ya * ya if sq is None else sq + ya * ya
        qa_ref[rows, :] = sq

    for c in range(bm // CHUNK):
        gmlp_tasks.append(partial(gmlp_chunk, c))

    def lru_setup():
        xe_ref[0:hist, :] = jnp.where(seq_start, jnp.zeros((hist, gw), _F32), hist_ref[jp])
        xe_ref[hist:hist + bm, :] = p_ref[3]

    def lru_subtile(n, r0):
        cols = slice(n * HEAD_DIM, (n + 1) * HEAD_DIM)
        rows = slice(r0, r0 + SCAN_ROWS)
        if r0 == 0:
            carry = jnp.where(seq_start, jnp.zeros((SUBLANES, HEAD_DIM), _F32), carry_ref[jp, :, cols])
        else:
            carry = carry_ref[jp, :, cols]
        neg = -lam_ref[:, cols]
        softplus = jnp.maximum(neg, 0.0) + jnp.log1p(jnp.exp(-jnp.abs(neg)))
        log2_a_per_r = (-LRU_C * _LOG2E) * softplus
        xc = bc_ref[:, cols]
        for k in range(CONV_WIDTH):
            tap = xe_ref[pl.ds(r0 + hist - (CONV_WIDTH - 1) + k, SCAN_ROWS), cols]
            xc = xc + tap * wc_ref[k:k + 1, cols]
        xc16 = xc.astype(_BF16)
        w_ri = jnp.concatenate([wr_ref[n], wi_ref[n]], axis=-1)
        ri = jnp.dot(xc16, w_ri, preferred_element_type=_F32)
        r = jax.nn.sigmoid(ri[:, :HEAD_DIM] + br_ref[:, cols])
        i = jax.nn.sigmoid(ri[:, HEAD_DIM:] + bi_ref[:, cols])
        a = jnp.exp2(r * log2_a_per_r)
        m2 = jnp.maximum(1.0 - a * a, 1e-12)
        bterm = (m2 * lax.rsqrt(m2)) * (i * xc)
        hseq, carry = _scan_rows(a, bterm, carry)
        carry_ref[jp, :, cols] = carry
        yb = hseq * _gelu(p_ref[2, rows, cols])
        yb_ref[rows, cols] = (yb * gb_ref[:, cols]).astype(yb_ref.dtype)
        if n == 0:
            qb_ref[rows, :] = yb * yb
        else:
            qb_ref[rows, :] += yb * yb

    def lru_finish():
        hist_ref[jp] = xe_ref[bm:bm + hist, :]

    for n in range(n_sub):
        for r0 in range(0, bm, SCAN_ROWS):
            lru_tasks.append(partial(lru_subtile, n, r0))
    return gmlp_tasks, lru_setup, lru_tasks, lru_finish


def _in_mix_kernel(x_ref, g_ref, wu_ref, wv_ref, wg_ref, wx_ref,
                   vg_ref, ws_ref, bs_ref, wc_ref, bc_ref, wr_ref, br_ref, wi_ref, bi_ref, lam_ref,
                   ga_ref, gb_ref, ya_ref, yb_ref, sa_ref, sb_ref,
                   hn_ref, pa_ref, pb_ref, qa_ref, qb_ref, xe_ref, hist_ref, carry_ref,
                   *, ng, blocks_per_seq):
    s = pl.program_id(0)
    n_mat = pl.num_programs(0) - 1
    j = lax.rem(jnp.minimum(s, n_mat - 1), ng)
    sp = jnp.maximum(s - 1, 0)
    ip, jp = sp // ng, lax.rem(sp, ng)
    seq_start = lax.rem(ip, blocks_per_seq) == 0

    @pl.when(s == 0)
    def _():
        pb_ref[...] = jnp.zeros(pb_ref.shape, _F32)
        hist_ref[...] = jnp.zeros(hist_ref.shape, _F32)
        carry_ref[...] = jnp.zeros(carry_ref.shape, _F32)

    @pl.when((j == 0) & (s < n_mat))
    def _():
        def norm_rows(rows):
            x = x_ref[rows, :]
            hn_ref[rows, :] = (x * _rms_scale(x) * g_ref[...]).astype(_BF16)
        _for_row_chunks(x_ref.shape[0], norm_rows)

    def step(mat_ref, mix_ref):
        bm = hn_ref.shape[0]
        mat_tasks = []
        for k, w_ref in enumerate((wu_ref, wv_ref, wg_ref, wx_ref)):
            for r0 in range(0, bm, MAT_ROWS):
                def mat_piece(k=k, w_ref=w_ref, r0=r0):
                    mat_ref[k, r0:r0 + MAT_ROWS, :] = jnp.dot(
                        hn_ref[r0:r0 + MAT_ROWS, :], w_ref[...], preferred_element_type=_F32)
                mat_tasks.append(mat_piece)
        gmlp_tasks, lru_setup, lru_tasks, lru_finish = _mix_tasks(
            mix_ref, jp, seq_start, vg_ref, ws_ref, bs_ref, wc_ref, bc_ref, wr_ref, br_ref,
            wi_ref, bi_ref, lam_ref, ga_ref, gb_ref, ya_ref, yb_ref, qa_ref, qb_ref,
            xe_ref, hist_ref, carry_ref)
        lru_setup()
        n = len(mat_tasks)
        for t in range(n):
            mat_tasks[t]()
            for task in gmlp_tasks[t * len(gmlp_tasks) // n:(t + 1) * len(gmlp_tasks) // n]:
                task()
            for task in lru_tasks[t * len(lru_tasks) // n:(t + 1) * len(lru_tasks) // n]:
                task()
        lru_finish()

    parity = lax.rem(s, 2)

    @pl.when(parity == 0)
    def _():
        step(pa_ref, pb_ref)

    @pl.when(parity == 1)
    def _():
        step(pb_ref, pa_ref)

    @pl.when(jp == 0)
    def _():
        sa_ref[...] = qa_ref[...]
        sb_ref[...] = qb_ref[...]

    @pl.when(jp > 0)
    def _():
        sa_ref[...] += qa_ref[...]
        sb_ref[...] += qb_ref[...]


def _in_mix(x, g, w, vg, ws, bs, wc, bc, wr, br, wi, bi, lam, ga, gb, *, seq, bm, gw):
    t, d = x.shape
    dh = vg.shape[1]
    ng = dh // gw
    n_sub = gw // HEAD_DIM
    n_mat = (t // bm) * ng

    def mat_idx(s):
        sm = jnp.minimum(s, n_mat - 1)
        return sm // ng, lax.rem(sm, ng)

    def mix_idx(s):
        sp = jnp.maximum(s - 1, 0)
        return sp // ng, lax.rem(sp, ng)

    def w_spec(c):
        return pl.BlockSpec((d, gw), lambda s, c=c: (0, c * ng + mat_idx(s)[1]))

    def vec(rows=1):
        return pl.BlockSpec((rows, gw), lambda s: (0, mix_idx(s)[1]))

    def per_head(shape):
        return pl.BlockSpec((n_sub,) + shape, lambda s: (mix_idx(s)[1], 0, 0))

    y_spec = pl.BlockSpec((bm, gw), lambda s: mix_idx(s))
    q_spec = pl.BlockSpec((bm, HEAD_DIM), lambda s: (mix_idx(s)[0], 0))
    return pl.pallas_call(
        partial(_in_mix_kernel, ng=ng, blocks_per_seq=seq // bm),
        grid=(n_mat + 1,),
        in_specs=[pl.BlockSpec((bm, d), lambda s: (mat_idx(s)[0], 0), pipeline_mode=_SINGLE),
                  pl.BlockSpec((1, d), lambda s: (0, 0)),
                  w_spec(0), w_spec(1), w_spec(2), w_spec(3),
                  vec(), per_head((CHUNK, CHUNK)), per_head((CHUNK, 1)),
                  vec(CONV_WIDTH), vec(),
                  per_head((HEAD_DIM, HEAD_DIM)), vec(),
                  per_head((HEAD_DIM, HEAD_DIM)), vec(), vec(),
                  vec(), vec()],
        out_specs=[y_spec, y_spec, q_spec, q_spec],
        out_shape=[jax.ShapeDtypeStruct((t, dh), _BF16), jax.ShapeDtypeStruct((t, dh), _BF16),
                   jax.ShapeDtypeStruct((t, HEAD_DIM), _F32), jax.ShapeDtypeStruct((t, HEAD_DIM), _F32)],
        scratch_shapes=[pltpu.VMEM((bm, d), _BF16),
                        pltpu.VMEM((4, bm, gw), _F32),
                        pltpu.VMEM((4, bm, gw), _F32),
                        pltpu.VMEM((bm, HEAD_DIM), _F32),
                        pltpu.VMEM((bm, HEAD_DIM), _F32),
                        pltpu.VMEM((bm + 2 * SUBLANES, gw), _F32),
                        pltpu.VMEM((ng, SUBLANES, gw), _F32),
                        pltpu.VMEM((ng, SUBLANES, gw), _F32)],
        compiler_params=_params(("arbitrary",)),
        name="in_proj_mixers",
    )(x, g, w, w, w, w, vg, ws, bs, wc, bc, wr, br, wi, bi, lam, ga, gb)


def _out_proj_kernel(ya_ref, yb_ref, qa_ref, qb_ref, w_ref, x_ref, gpost_ref, gpre_ref,
                     x1_ref, h2_ref, *, bn):
    da = ya_ref.shape[1]
    db = yb_ref.shape[1]
    scale_a = lax.rsqrt(jnp.sum(qa_ref[...], axis=-1, keepdims=True) * (1.0 / da) + EPS)
    scale_b = lax.rsqrt(jnp.sum(qb_ref[...], axis=-1, keepdims=True) * (1.0 / db) + EPS)
    ya = ya_ref[...]
    yb = yb_ref[...]
    for n in range(w_ref.shape[1] // bn):
        cols = slice(n * bn, (n + 1) * bn)
        x1_ref[:, cols] = (
            scale_a * jnp.dot(ya, w_ref[0:da, cols], preferred_element_type=_F32)
            + scale_b * jnp.dot(yb, w_ref[da:da + db, cols], preferred_element_type=_F32))

    def rows_step(rows):
        y = x1_ref[rows, :]
        x1 = x_ref[rows, :] + y * _rms_scale(y) * gpost_ref[...]
        x1_ref[rows, :] = x1
        h2_ref[rows, :] = (x1 * _rms_scale(x1) * gpre_ref[...]).astype(h2_ref.dtype)
    _for_row_chunks(x1_ref.shape[0], rows_step)


def _out_proj(ya, yb, qa, qb, w, x, gpost, gpre, *, bm, bn):
    t, da = ya.shape
    db = yb.shape[1]
    d = w.shape[1]
    row = lambda i: (i, 0)
    const = lambda i: (0, 0)
    return pl.pallas_call(
        partial(_out_proj_kernel, bn=bn),
        grid=(t // bm,),
        in_specs=[pl.BlockSpec((bm, da), row),
                  pl.BlockSpec((bm, db), row),
                  pl.BlockSpec((bm, HEAD_DIM), row),
                  pl.BlockSpec((bm, HEAD_DIM), row),
                  pl.BlockSpec((da + db, d), const),
                  pl.BlockSpec((bm, d), row),
                  pl.BlockSpec((1, d), const),
                  pl.BlockSpec((1, d), const)],
        out_specs=[pl.BlockSpec((bm, d), row), pl.BlockSpec((bm, d), row)],
        out_shape=[jax.ShapeDtypeStruct((t, d), _F32), jax.ShapeDtypeStruct((t, d), _BF16)],
        compiler_params=_params(("arbitrary",)),
        name="out_proj_resid",
    )(ya, yb, qa, qb, w, x, gpost, gpre)


def _ffn_kernel(h_ref, x1_ref, wg_ref, wu_ref, wo_ref, gpost_ref, o_ref, *, nf):
    j = pl.program_id(1)

    @pl.when(j == 0)
    def _():
        o_ref[...] = jnp.zeros(o_ref.shape, _F32)

    @pl.when(j < nf)
    def _():
        wg = wg_ref[...].astype(_BF16)
        wu = wu_ref[...].astype(_BF16)
        wo = wo_ref[...].astype(_BF16)
        n_rb = h_ref.shape[0] // FFN_ROWS

        def gate_up(rb):
            h = h_ref[rb * FFN_ROWS:(rb + 1) * FFN_ROWS, :]
            return (jnp.dot(h, wg, preferred_element_type=_F32),
                    jnp.dot(h, wu, preferred_element_type=_F32))

        def down(rb, gate, up):
            act = (jax.nn.silu(gate) * up).astype(_BF16)
            o_ref[rb * FFN_ROWS:(rb + 1) * FFN_ROWS, :] += jnp.dot(act, wo, preferred_element_type=_F32)

        pending = gate_up(0)
        for rb in range(1, n_rb):
            nxt = gate_up(rb)
            down(rb - 1, *pending)
            pending = nxt
        down(n_rb - 1, *pending)

    @pl.when(j >= nf)
    def _():
        slice_rows = x1_ref.shape[0]
        base = (j - nf) * slice_rows

        def rows_step(rows):
            out_rows = pl.ds(pl.multiple_of(base + rows.start, ROW_CHUNK), ROW_CHUNK)
            f = o_ref[out_rows, :]
            o_ref[out_rows, :] = x1_ref[rows, :] + f * _rms_scale(f) * gpost_ref[...]
        _for_row_chunks(slice_rows, rows_step)


def _ffn(h2, x1, w_in, w_out, gpost, *, bm, bf, n_epi):
    t, d = h2.shape
    f = w_out.shape[0]
    nf = f // bf
    last = nf - 1
    return pl.pallas_call(
        partial(_ffn_kernel, nf=nf),
        grid=(t // bm, nf + n_epi),
        in_specs=[pl.BlockSpec((bm, d), lambda i, j: (i, 0), pipeline_mode=_SINGLE),
                  pl.BlockSpec((bm // n_epi, d),
                               lambda i, j: (i * n_epi + jnp.clip(j - nf, 0, n_epi - 1), 0)),
                  pl.BlockSpec((d, bf), lambda i, j: (0, jnp.minimum(j, last))),
                  pl.BlockSpec((d, bf), lambda i, j: (0, nf + jnp.minimum(j, last))),
                  pl.BlockSpec((bf, d), lambda i, j: (jnp.minimum(j, last), 0)),
                  pl.BlockSpec((1, d), lambda i, j: (0, 0))],
        out_specs=pl.BlockSpec((bm, d), lambda i, j: (i, 0), pipeline_mode=_SINGLE),
        out_shape=jax.ShapeDtypeStruct((t, d), _F32),
        compiler_params=_params(("arbitrary", "arbitrary")),
        name="swiglu_ffn",
    )(h2, x1, w_in, w_in, w_out, gpost)


def kernel(x, pre_mix_g, w_in, gmlp_v_norm_g, w_spatial, b_spatial, w_conv, b_conv, w_r, b_r, w_i, b_i, lru_lambda, out_norm_a_g, out_norm_b_g, w_out, post_mix_g, pre_ffn_g, w_ffn_in, w_ffn_out, post_ffn_g):
    batch, seq, d = x.shape
    depth = w_in.shape[0]
    xt = x.reshape(batch * seq, d)
    for l in range(depth):
        ya, yb, qa, qb = _in_mix(
            xt, pre_mix_g[l][None], w_in[l].astype(_BF16), gmlp_v_norm_g[l][None], w_spatial[l],
            b_spatial[l][:, :, None], w_conv[l], b_conv[l][None], w_r[l].astype(_BF16), b_r[l][None],
            w_i[l].astype(_BF16), b_i[l][None], lru_lambda[l][None],
            out_norm_a_g[l][None], out_norm_b_g[l][None], seq=seq, bm=1024, gw=256)
        x1, h2 = _out_proj(ya, yb, qa, qb, w_out[l].astype(_BF16), xt, post_mix_g[l][None],
                           pre_ffn_g[l][None], bm=256, bn=512)
        xt = _ffn(h2, x1, w_ffn_in[l], w_ffn_out[l], post_ffn_g[l][None],
                  bm=1024, bf=256, n_epi=8)
    return xt.reshape(batch, seq, d)
```

```python
import math
from functools import partial

import jax
import jax.numpy as jnp
from jax import lax
from jax.experimental import pallas as pl
from jax.experimental.pallas import tpu as pltpu

EPS = 1e-6
HEAD_DIM = 128
CHUNK = 128
CONV_WIDTH = 4
LRU_C = 8.0
SUBLANES = 8
ROW_CHUNK = 32
GROUP_ROWS = 128
SCAN_ROWS = 128
MAT_ROWS = 256
FFN_ROWS = 256
VMEM_LIMIT_BYTES = 60 * 1024 * 1024

_BF16 = jnp.bfloat16
_F32 = jnp.float32
_LOG2E = math.log2(math.e)
_GELU_A = -2.0 * math.sqrt(2.0 / math.pi) * _LOG2E
_GELU_B = _GELU_A * 0.044715


def _gelu(x):
    return x / (1.0 + jnp.exp2(x * (_GELU_A + _GELU_B * (x * x))))


def _rms_scale(x):
    return lax.rsqrt(jnp.mean(x * x, axis=-1, keepdims=True) + EPS)


def _for_row_groups(n_rows, body):
    def step(i, carry):
        row0 = i * GROUP_ROWS
        body([pl.ds(pl.multiple_of(row0 + c * ROW_CHUNK, ROW_CHUNK), ROW_CHUNK)
              for c in range(GROUP_ROWS // ROW_CHUNK)])
        return carry
    lax.fori_loop(0, n_rows // GROUP_ROWS, step, 0)


def _params(semantics):
    return pltpu.CompilerParams(dimension_semantics=semantics,
                                vmem_limit_bytes=VMEM_LIMIT_BYTES)


_SINGLE = pl.Buffered(1)


def _scan_rows(a, b, carry):
    tm = a.shape[0]
    sub = lax.broadcasted_iota(jnp.int32, (SUBLANES, a.shape[1]), 0)
    hs = []
    for g in range(tm // SUBLANES):
        sl = slice(g * SUBLANES, (g + 1) * SUBLANES)
        ag, bg = a[sl], b[sl]
        d = 1
        while d < SUBLANES:
            keep = sub >= d
            a_sh = jnp.where(keep, pltpu.roll(ag, d, axis=0), 1.0)
            b_sh = jnp.where(keep, pltpu.roll(bg, d, axis=0), 0.0)
            bg = ag * b_sh + bg
            ag = ag * a_sh
            d *= 2
        h = ag * carry + bg
        hs.append(h)
        carry = jnp.broadcast_to(h[SUBLANES - 1:SUBLANES, :], h.shape)
    return jnp.concatenate(hs, axis=0), carry


def _mix_tasks(p_ref, jp, seq_start, vg_ref, ws_ref, bs_ref, wc_ref, bc_ref, wr_ref, br_ref,
               wi_ref, bi_ref, lam_ref, ga_ref, gb_ref, ya_ref, yb_ref, qa_ref, qb_ref,
               xe_ref, hist_ref, carry_ref):
    _, bm, gw = p_ref.shape
    hist = SUBLANES
    n_sub = gw // HEAD_DIM
    gmlp_tasks, lru_tasks = [], []

    def gmlp_chunk(c):
        tril = (lax.broadcasted_iota(jnp.int32, (CHUNK, CHUNK), 0)
                >= lax.broadcasted_iota(jnp.int32, (CHUNK, CHUNK), 1))
        rows = slice(c * CHUNK, (c + 1) * CHUNK)
        sq = None
        for h in range(n_sub):
            cols = slice(h * HEAD_DIM, (h + 1) * HEAD_DIM)
            ws = jnp.where(tril, ws_ref[h], 0.0).astype(_BF16)
            gv = _gelu(p_ref[1, rows, cols])
            vn = (gv * _rms_scale(gv) * vg_ref[:, cols]).astype(_BF16)
            mixed = (jnp.dot(ws, vn, preferred_element_type=_F32)
                     + jnp.broadcast_to(bs_ref[h], (CHUNK, HEAD_DIM)))
            ya = _gelu(p_ref[0, rows, cols]) * mixed
            ya_ref[rows, cols] = (ya * ga_ref[:, cols]).astype(ya_ref.dtype)
            sq = ya * ya if sq is None else sq + ya * ya
        qa_ref[rows, :] = sq

    for c in range(bm // CHUNK):
        gmlp_tasks.append(partial(gmlp_chunk, c))

    def lru_setup():
        xe_ref[0:hist, :] = jnp.where(seq_start, jnp.zeros((hist, gw), _F32), hist_ref[jp])
        xe_ref[hist:hist + bm, :] = p_ref[3]

    def lru_subtile(n, r0):
        cols = slice(n * HEAD_DIM, (n + 1) * HEAD_DIM)
        rows = slice(r0, r0 + SCAN_ROWS)
        if r0 == 0:
            carry = jnp.where(seq_start, jnp.zeros((SUBLANES, HEAD_DIM), _F32), carry_ref[jp, :, cols])
        else:
            carry = carry_ref[jp, :, cols]
        neg = -lam_ref[:, cols]
        softplus = jnp.maximum(neg, 0.0) + jnp.log1p(jnp.exp(-jnp.abs(neg)))
        log2_a_per_r = (-LRU_C * _LOG2E) * softplus
        xc = bc_ref[:, cols]
        for k in range(CONV_WIDTH):
            tap = xe_ref[pl.ds(r0 + hist - (CONV_WIDTH - 1) + k, SCAN_ROWS), cols]
            xc = xc + tap * wc_ref[k:k + 1, cols]
        xc16 = xc.astype(_BF16)
        w_ri = jnp.concatenate([wr_ref[n], wi_ref[n]], axis=-1)
        ri = jnp.dot(xc16, w_ri, preferred_element_type=_F32)
        r = jax.nn.sigmoid(ri[:, :HEAD_DIM] + br_ref[:, cols])
        i = jax.nn.sigmoid(ri[:, HEAD_DIM:] + bi_ref[:, cols])
        a = jnp.exp2(r * log2_a_per_r)
        m2 = jnp.maximum(1.0 - a * a, 1e-12)
        bterm = (m2 * lax.rsqrt(m2)) * (i * xc)
        hseq, carry = _scan_rows(a, bterm, carry)
        carry_ref[jp, :, cols] = carry
        yb = hseq * _gelu(p_ref[2, rows, cols])
        yb_ref[rows, cols] = (yb * gb_ref[:, cols]).astype(yb_ref.dtype)
        if n == 0:
            qb_ref[rows, :] = yb * yb
        else:
            qb_ref[rows, :] += yb * yb

    def lru_finish():
        hist_ref[jp] = xe_ref[bm:bm + hist, :]

    for n in range(n_sub):
        for r0 in range(0, bm, SCAN_ROWS):
            lru_tasks.append(partial(lru_subtile, n, r0))
    return gmlp_tasks, lru_setup, lru_tasks, lru_finish


def _in_mix_kernel(x_ref, g_ref, wu_ref, wv_ref, wg_ref, wx_ref,
                   vg_ref, ws_ref, bs_ref, wc_ref, bc_ref, wr_ref, br_ref, wi_ref, bi_ref, lam_ref,
                   ga_ref, gb_ref, ya_ref, yb_ref, sa_ref, sb_ref,
                   hn_ref, pa_ref, pb_ref, qa_ref, qb_ref, xe_ref, hist_ref, carry_ref,
                   *, ng, blocks_per_seq):
    s = pl.program_id(0)
    n_mat = pl.num_programs(0) - 1
    j = lax.rem(jnp.minimum(s, n_mat - 1), ng)
    sp = jnp.maximum(s - 1, 0)
    ip, jp = sp // ng, lax.rem(sp, ng)
    seq_start = lax.rem(ip, blocks_per_seq) == 0

    @pl.when(s == 0)
    def _():
        pb_ref[...] = jnp.zeros(pb_ref.shape, _F32)
        hist_ref[...] = jnp.zeros(hist_ref.shape, _F32)
        carry_ref[...] = jnp.zeros(carry_ref.shape, _F32)

    @pl.when((j == 0) & (s < n_mat))
    def _():
        def norm_group(chunks):
            scales = [_rms_scale(x_ref[rows, :]) for rows in chunks]
            for rows, scale in zip(chunks, scales):
                hn_ref[rows, :] = (x_ref[rows, :] * scale * g_ref[...]).astype(_BF16)
        _for_row_groups(x_ref.shape[0], norm_group)

    def step(mat_ref, mix_ref):
        bm = hn_ref.shape[0]
        mat_tasks = []
        for k, w_ref in enumerate((wu_ref, wv_ref, wg_ref, wx_ref)):
            for r0 in range(0, bm, MAT_ROWS):
                def mat_piece(k=k, w_ref=w_ref, r0=r0):
                    mat_ref[k, r0:r0 + MAT_ROWS, :] = jnp.dot(
                        hn_ref[r0:r0 + MAT_ROWS, :], w_ref[...], preferred_element_type=_F32)
                mat_tasks.append(mat_piece)
        gmlp_tasks, lru_setup, lru_tasks, lru_finish = _mix_tasks(
            mix_ref, jp, seq_start, vg_ref, ws_ref, bs_ref, wc_ref, bc_ref, wr_ref, br_ref,
            wi_ref, bi_ref, lam_ref, ga_ref, gb_ref, ya_ref, yb_ref, qa_ref, qb_ref,
            xe_ref, hist_ref, carry_ref)
        lru_setup()
        n = len(mat_tasks)
        for t in range(n):
            mat_tasks[t]()
            for task in gmlp_tasks[t * len(gmlp_tasks) // n:(t + 1) * len(gmlp_tasks) // n]:
                task()
            for task in lru_tasks[t * len(lru_tasks) // n:(t + 1) * len(lru_tasks) // n]:
                task()
        lru_finish()

    parity = lax.rem(s, 2)

    @pl.when(parity == 0)
    def _():
        step(pa_ref, pb_ref)

    @pl.when(parity == 1)
    def _():
        step(pb_ref, pa_ref)

    @pl.when(jp == 0)
    def _():
        sa_ref[...] = qa_ref[...]
        sb_ref[...] = qb_ref[...]

    @pl.when(jp > 0)
    def _():
        sa_ref[...] += qa_ref[...]
        sb_ref[...] += qb_ref[...]


def _in_mix(x, g, w, vg, ws, bs, wc, bc, wr, br, wi, bi, lam, ga, gb, *, seq, bm, gw):
    t, d = x.shape
    dh = vg.shape[1]
    ng = dh // gw
    n_sub = gw // HEAD_DIM
    n_mat = (t // bm) * ng

    def mat_idx(s):
        sm = jnp.minimum(s, n_mat - 1)
        return sm // ng, lax.rem(sm, ng)

    def mix_idx(s):
        sp = jnp.maximum(s - 1, 0)
        return sp // ng, lax.rem(sp, ng)

    def w_spec(c):
        return pl.BlockSpec((d, gw), lambda s, c=c: (0, c * ng + mat_idx(s)[1]))

    def vec(rows=1):
        return pl.BlockSpec((rows, gw), lambda s: (0, mix_idx(s)[1]))

    def per_head(shape):
        return pl.BlockSpec((n_sub,) + shape, lambda s: (mix_idx(s)[1], 0, 0))

    y_spec = pl.BlockSpec((bm, gw), lambda s: mix_idx(s))
    q_spec = pl.BlockSpec((bm, HEAD_DIM), lambda s: (mix_idx(s)[0], 0))
    return pl.pallas_call(
        partial(_in_mix_kernel, ng=ng, blocks_per_seq=seq // bm),
        grid=(n_mat + 1,),
        in_specs=[pl.BlockSpec((bm, d), lambda s: (mat_idx(s)[0], 0), pipeline_mode=_SINGLE),
                  pl.BlockSpec((1, d), lambda s: (0, 0)),
                  w_spec(0), w_spec(1), w_spec(2), w_spec(3),
                  vec(), per_head((CHUNK, CHUNK)), per_head((CHUNK, 1)),
                  vec(CONV_WIDTH), vec(),
                  per_head((HEAD_DIM, HEAD_DIM)), vec(),
                  per_head((HEAD_DIM, HEAD_DIM)), vec(), vec(),
                  vec(), vec()],
        out_specs=[y_spec, y_spec, q_spec, q_spec],
        out_shape=[jax.ShapeDtypeStruct((t, dh), _BF16), jax.ShapeDtypeStruct((t, dh), _BF16),
                   jax.ShapeDtypeStruct((t, HEAD_DIM), _F32), jax.ShapeDtypeStruct((t, HEAD_DIM), _F32)],
        scratch_shapes=[pltpu.VMEM((bm, d), _BF16),
                        pltpu.VMEM((4, bm, gw), _F32),
                        pltpu.VMEM((4, bm, gw), _F32),
                        pltpu.VMEM((bm, HEAD_DIM), _F32),
                        pltpu.VMEM((bm, HEAD_DIM), _F32),
                        pltpu.VMEM((bm + 2 * SUBLANES, gw), _F32),
                        pltpu.VMEM((ng, SUBLANES, gw), _F32),
                        pltpu.VMEM((ng, SUBLANES, gw), _F32)],
        compiler_params=_params(("arbitrary",)),
        name="in_proj_mixers",
    )(x, g, w, w, w, w, vg, ws, bs, wc, bc, wr, br, wi, bi, lam, ga, gb)


def _out_proj_kernel(ya_ref, yb_ref, qa_ref, qb_ref, w_ref, x_ref, gpost_ref, gpre_ref,
                     x1_ref, h2_ref, *, bn):
    da = ya_ref.shape[1]
    db = yb_ref.shape[1]
    scale_a = lax.rsqrt(jnp.sum(qa_ref[...], axis=-1, keepdims=True) * (1.0 / da) + EPS)
    scale_b = lax.rsqrt(jnp.sum(qb_ref[...], axis=-1, keepdims=True) * (1.0 / db) + EPS)
    ya = ya_ref[...]
    yb = yb_ref[...]
    for n in range(w_ref.shape[1] // bn):
        cols = slice(n * bn, (n + 1) * bn)
        x1_ref[:, cols] = (
            scale_a * jnp.dot(ya, w_ref[0:da, cols], preferred_element_type=_F32)
            + scale_b * jnp.dot(yb, w_ref[da:da + db, cols], preferred_element_type=_F32))

    def resid_group(chunks):
        y_scales = [_rms_scale(x1_ref[rows, :]) for rows in chunks]
        x1_scales = []
        for rows, scale in zip(chunks, y_scales):
            x1 = x_ref[rows, :] + x1_ref[rows, :] * scale * gpost_ref[...]
            x1_ref[rows, :] = x1
            x1_scales.append(_rms_scale(x1))
        for rows, scale in zip(chunks, x1_scales):
            h2_ref[rows, :] = (x1_ref[rows, :] * scale * gpre_ref[...]).astype(h2_ref.dtype)
    _for_row_groups(x1_ref.shape[0], resid_group)


def _out_proj(ya, yb, qa, qb, w, x, gpost, gpre, *, bm, bn):
    t, da = ya.shape
    db = yb.shape[1]
    d = w.shape[1]
    row = lambda i: (i, 0)
    const = lambda i: (0, 0)
    return pl.pallas_call(
        partial(_out_proj_kernel, bn=bn),
        grid=(t // bm,),
        in_specs=[pl.BlockSpec((bm, da), row),
                  pl.BlockSpec((bm, db), row),
                  pl.BlockSpec((bm, HEAD_DIM), row),
                  pl.BlockSpec((bm, HEAD_DIM), row),
                  pl.BlockSpec((da + db, d), const),
                  pl.BlockSpec((bm, d), row),
                  pl.BlockSpec((1, d), const),
                  pl.BlockSpec((1, d), const)],
        out_specs=[pl.BlockSpec((bm, d), row), pl.BlockSpec((bm, d), row)],
        out_shape=[jax.ShapeDtypeStruct((t, d), _F32), jax.ShapeDtypeStruct((t, d), _BF16)],
        compiler_params=_params(("arbitrary",)),
        name="out_proj_resid",
    )(ya, yb, qa, qb, w, x, gpost, gpre)


def _ffn_kernel(h_ref, x1_ref, wg_ref, wu_ref, wo_ref, gpost_ref, o_ref, *, nf):
    j = pl.program_id(1)

    @pl.when(j == 0)
    def _():
        o_ref[...] = jnp.zeros(o_ref.shape, _F32)

    @pl.when(j < nf)
    def _():
        wg = wg_ref[...].astype(_BF16)
        wu = wu_ref[...].astype(_BF16)
        wo = wo_ref[...].astype(_BF16)
        n_rb = h_ref.shape[0] // FFN_ROWS

        def gate_up(rb):
            h = h_ref[rb * FFN_ROWS:(rb + 1) * FFN_ROWS, :]
            return (jnp.dot(h, wg, preferred_element_type=_F32),
                    jnp.dot(h, wu, preferred_element_type=_F32))

        def down(rb, gate, up):
            act = (jax.nn.silu(gate) * up).astype(_BF16)
            o_ref[rb * FFN_ROWS:(rb + 1) * FFN_ROWS, :] += jnp.dot(act, wo, preferred_element_type=_F32)

        pending = gate_up(0)
        for rb in range(1, n_rb):
            nxt = gate_up(rb)
            down(rb - 1, *pending)
            pending = nxt
        down(n_rb - 1, *pending)

    @pl.when(j >= nf)
    def _():
        slice_rows = x1_ref.shape[0]
        base = (j - nf) * slice_rows

        def resid_group(chunks):
            out_chunks = [pl.ds(pl.multiple_of(base + rows.start, ROW_CHUNK), ROW_CHUNK) for rows in chunks]
            scales = [_rms_scale(o_ref[rows, :]) for rows in out_chunks]
            for rows, out_rows, scale in zip(chunks, out_chunks, scales):
                o_ref[out_rows, :] = x1_ref[rows, :] + o_ref[out_rows, :] * scale * gpost_ref[...]
        _for_row_groups(slice_rows, resid_group)


def _ffn(h2, x1, w_in, w_out, gpost, *, bm, bf, n_epi):
    t, d = h2.shape
    f = w_out.shape[0]
    nf = f // bf
    last = nf - 1
    return pl.pallas_call(
        partial(_ffn_kernel, nf=nf),
        grid=(t // bm, nf + n_epi),
        in_specs=[pl.BlockSpec((bm, d), lambda i, j: (i, 0), pipeline_mode=_SINGLE),
                  pl.BlockSpec((bm // n_epi, d),
                               lambda i, j: (i * n_epi + jnp.clip(j - nf, 0, n_epi - 1), 0)),
                  pl.BlockSpec((d, bf), lambda i, j: (0, jnp.minimum(j, last))),
                  pl.BlockSpec((d, bf), lambda i, j: (0, nf + jnp.minimum(j, last))),
                  pl.BlockSpec((bf, d), lambda i, j: (jnp.minimum(j, last), 0)),
                  pl.BlockSpec((1, d), lambda i, j: (0, 0))],
        out_specs=pl.BlockSpec((bm, d), lambda i, j: (i, 0), pipeline_mode=_SINGLE),
        out_shape=jax.ShapeDtypeStruct((t, d), _F32),
        compiler_params=_params(("arbitrary", "arbitrary")),
        name="swiglu_ffn",
    )(h2, x1, w_in, w_in, w_out, gpost)


def kernel(x, pre_mix_g, w_in, gmlp_v_norm_g, w_spatial, b_spatial, w_conv, b_conv, w_r, b_r, w_i, b_i, lru_lambda, out_norm_a_g, out_norm_b_g, w_out, post_mix_g, pre_ffn_g, w_ffn_in, w_ffn_out, post_ffn_g):
    batch, seq, d = x.shape
    depth = w_in.shape[0]
    xt = x.reshape(batch * seq, d)
    for l in range(depth):
        ya, yb, qa, qb = _in_mix(
            xt, pre_mix_g[l][None], w_in[l].astype(_BF16), gmlp_v_norm_g[l][None], w_spatial[l],
            b_spatial[l][:, :, None], w_conv[l], b_conv[l][None], w_r[l].astype(_BF16), b_r[l][None],
            w_i[l].astype(_BF16), b_i[l][None], lru_lambda[l][None],
            out_norm_a_g[l][None], out_norm_b_g[l][None], seq=seq, bm=1024, gw=256)
        x1, h2 = _out_proj(ya, yb, qa, qb, w_out[l].astype(_BF16), xt, post_mix_g[l][None],
                           pre_ffn_g[l][None], bm=256, bn=512)
        xt = _ffn(h2, x1, w_ffn_in[l], w_ffn_out[l], post_ffn_g[l][None],
                  bm=1024, bf=256, n_epi=8)
    return xt.reshape(batch, seq, d)
```

```python
import math
from functools import partial

import jax
import jax.numpy as jnp
from jax import lax
from jax.experimental import pallas as pl
from jax.experimental.pallas import tpu as pltpu

EPS = 1e-6
HEAD_DIM = 128
CHUNK = 128
CONV_WIDTH = 4
LRU_C = 8.0
SUBLANES = 8
ROW_CHUNK = 32
GROUP_ROWS = 128
SCAN_ROWS = 128
MAT_ROWS = 256
FFN_ROWS = 256
VMEM_LIMIT_BYTES = 60 * 1024 * 1024

_BF16 = jnp.bfloat16
_F32 = jnp.float32
_LOG2E = math.log2(math.e)
_GELU_A = -2.0 * math.sqrt(2.0 / math.pi) * _LOG2E
_GELU_B = _GELU_A * 0.044715


def _gelu(x):
    return x / (1.0 + jnp.exp2(x * (_GELU_A + _GELU_B * (x * x))))


def _rms_scale(x):
    return lax.rsqrt(jnp.mean(x * x, axis=-1, keepdims=True) + EPS)


def _for_row_groups(n_rows, body):
    def step(i, carry):
        row0 = i * GROUP_ROWS
        body([pl.ds(pl.multiple_of(row0 + c * ROW_CHUNK, ROW_CHUNK), ROW_CHUNK)
              for c in range(GROUP_ROWS // ROW_CHUNK)])
        return carry
    lax.fori_loop(0, n_rows // GROUP_ROWS, step, 0)


def _params(semantics):
    return pltpu.CompilerParams(dimension_semantics=semantics,
                                vmem_limit_bytes=VMEM_LIMIT_BYTES)


_ROW_VG, _ROW_WC, _ROW_BC, _ROW_BR, _ROW_BI, _ROW_LAM, _ROW_GA, _ROW_GB = 0, 1, 5, 6, 7, 8, 9, 10
_VEC_ROWS = 16

_SINGLE = pl.Buffered(1)


def _scan_rows(a, b, carry):
    tm = a.shape[0]
    sub = lax.broadcasted_iota(jnp.int32, (SUBLANES, a.shape[1]), 0)
    hs = []
    for g in range(tm // SUBLANES):
        sl = slice(g * SUBLANES, (g + 1) * SUBLANES)
        ag, bg = a[sl], b[sl]
        d = 1
        while d < SUBLANES:
            keep = sub >= d
            a_sh = jnp.where(keep, pltpu.roll(ag, d, axis=0), 1.0)
            b_sh = jnp.where(keep, pltpu.roll(bg, d, axis=0), 0.0)
            bg = ag * b_sh + bg
            ag = ag * a_sh
            d *= 2
        h = ag * carry + bg
        hs.append(h)
        carry = jnp.broadcast_to(h[SUBLANES - 1:SUBLANES, :], h.shape)
    return jnp.concatenate(hs, axis=0), carry


def _mix_tasks(p_ref, jp, seq_start, vec_ref, ws_ref, bs_ref, wri_ref, ya_ref, yb_ref, qa_ref, qb_ref,
               xe_ref, hist_ref, carry_ref):
    def vec(row, cols):
        return vec_ref[row:row + 1, cols]

    _, bm, gw = p_ref.shape
    hist = SUBLANES
    n_sub = gw // HEAD_DIM
    gmlp_tasks, lru_tasks = [], []

    def gmlp_chunk(c):
        tril = (lax.broadcasted_iota(jnp.int32, (CHUNK, CHUNK), 0)
                >= lax.broadcasted_iota(jnp.int32, (CHUNK, CHUNK), 1))
        rows = slice(c * CHUNK, (c + 1) * CHUNK)
        sq = None
        for h in range(n_sub):
            cols = slice(h * HEAD_DIM, (h + 1) * HEAD_DIM)
            ws = jnp.where(tril, ws_ref[h], 0.0).astype(_BF16)
            gv = _gelu(p_ref[1, rows, cols])
            vn = (gv * _rms_scale(gv) * vec(_ROW_VG, cols)).astype(_BF16)
            mixed = (jnp.dot(ws, vn, preferred_element_type=_F32)
                     + jnp.broadcast_to(bs_ref[h], (CHUNK, HEAD_DIM)))
            ya = _gelu(p_ref[0, rows, cols]) * mixed
            ya_ref[rows, cols] = (ya * vec(_ROW_GA, cols)).astype(ya_ref.dtype)
            sq = ya * ya if sq is None else sq + ya * ya
        qa_ref[rows, :] = sq

    for c in range(bm // CHUNK):
        gmlp_tasks.append(partial(gmlp_chunk, c))

    def lru_setup():
        xe_ref[0:hist, :] = jnp.where(seq_start, jnp.zeros((hist, gw), _F32), hist_ref[jp])
        xe_ref[hist:hist + bm, :] = p_ref[3]

    def lru_subtile(n, r0):
        cols = slice(n * HEAD_DIM, (n + 1) * HEAD_DIM)
        rows = slice(r0, r0 + SCAN_ROWS)
        if r0 == 0:
            carry = jnp.where(seq_start, jnp.zeros((SUBLANES, HEAD_DIM), _F32), carry_ref[jp, :, cols])
        else:
            carry = carry_ref[jp, :, cols]
        neg = -vec(_ROW_LAM, cols)
        softplus = jnp.maximum(neg, 0.0) + jnp.log1p(jnp.exp(-jnp.abs(neg)))
        log2_a_per_r = (-LRU_C * _LOG2E) * softplus
        xc = vec(_ROW_BC, cols)
        for k in range(CONV_WIDTH):
            tap = xe_ref[pl.ds(r0 + hist - (CONV_WIDTH - 1) + k, SCAN_ROWS), cols]
            xc = xc + tap * vec(_ROW_WC + k, cols)
        ri = jnp.dot(xc.astype(_BF16), wri_ref[n], preferred_element_type=_F32)
        r = jax.nn.sigmoid(ri[:, :HEAD_DIM] + vec(_ROW_BR, cols))
        i = jax.nn.sigmoid(ri[:, HEAD_DIM:] + vec(_ROW_BI, cols))
        a = jnp.exp2(r * log2_a_per_r)
        m2 = jnp.maximum(1.0 - a * a, 1e-12)
        bterm = (m2 * lax.rsqrt(m2)) * (i * xc)
        hseq, carry = _scan_rows(a, bterm, carry)
        carry_ref[jp, :, cols] = carry
        yb = hseq * _gelu(p_ref[2, rows, cols])
        yb_ref[rows, cols] = (yb * vec(_ROW_GB, cols)).astype(yb_ref.dtype)
        if n == 0:
            qb_ref[rows, :] = yb * yb
        else:
            qb_ref[rows, :] += yb * yb

    def lru_finish():
        hist_ref[jp] = xe_ref[bm:bm + hist, :]

    for n in range(n_sub):
        for r0 in range(0, bm, SCAN_ROWS):
            lru_tasks.append(partial(lru_subtile, n, r0))
    return gmlp_tasks, lru_setup, lru_tasks, lru_finish


def _in_mix_kernel(x_ref, g_ref, wu_ref, wv_ref, wg_ref, wx_ref,
                   vec_ref, ws_ref, bs_ref, wri_ref, ya_ref, yb_ref, sa_ref, sb_ref,
                   hn_ref, pa_ref, pb_ref, qa_ref, qb_ref, xe_ref, hist_ref, carry_ref,
                   *, ng, blocks_per_seq):
    s = pl.program_id(0)
    n_mat = pl.num_programs(0) - 1
    j = lax.rem(jnp.minimum(s, n_mat - 1), ng)
    sp = jnp.maximum(s - 1, 0)
    ip, jp = sp // ng, lax.rem(sp, ng)
    seq_start = lax.rem(ip, blocks_per_seq) == 0

    @pl.when(s == 0)
    def _():
        pb_ref[...] = jnp.zeros(pb_ref.shape, _F32)
        hist_ref[...] = jnp.zeros(hist_ref.shape, _F32)
        carry_ref[...] = jnp.zeros(carry_ref.shape, _F32)

    @pl.when((j == 0) & (s < n_mat))
    def _():
        def norm_group(chunks):
            scales = [_rms_scale(x_ref[rows, :]) for rows in chunks]
            for rows, scale in zip(chunks, scales):
                hn_ref[rows, :] = (x_ref[rows, :] * scale * g_ref[...]).astype(_BF16)
        _for_row_groups(x_ref.shape[0], norm_group)

    def step(mat_ref, mix_ref):
        bm = hn_ref.shape[0]
        mat_tasks = []
        for k, w_ref in enumerate((wu_ref, wv_ref, wg_ref, wx_ref)):
            for r0 in range(0, bm, MAT_ROWS):
                def mat_piece(k=k, w_ref=w_ref, r0=r0):
                    mat_ref[k, r0:r0 + MAT_ROWS, :] = jnp.dot(
                        hn_ref[r0:r0 + MAT_ROWS, :], w_ref[...], preferred_element_type=_F32)
                mat_tasks.append(mat_piece)
        gmlp_tasks, lru_setup, lru_tasks, lru_finish = _mix_tasks(
            mix_ref, jp, seq_start, vec_ref, ws_ref, bs_ref, wri_ref, ya_ref, yb_ref, qa_ref, qb_ref,
            xe_ref, hist_ref, carry_ref)
        lru_setup()
        n = len(mat_tasks)
        for t in range(n):
            mat_tasks[t]()
            for task in gmlp_tasks[t * len(gmlp_tasks) // n:(t + 1) * len(gmlp_tasks) // n]:
                task()
            for task in lru_tasks[t * len(lru_tasks) // n:(t + 1) * len(lru_tasks) // n]:
                task()
        lru_finish()

    parity = lax.rem(s, 2)

    @pl.when(parity == 0)
    def _():
        step(pa_ref, pb_ref)

    @pl.when(parity == 1)
    def _():
        step(pb_ref, pa_ref)

    @pl.when(jp == 0)
    def _():
        sa_ref[...] = qa_ref[...]
        sb_ref[...] = qb_ref[...]

    @pl.when(jp > 0)
    def _():
        sa_ref[...] += qa_ref[...]
        sb_ref[...] += qb_ref[...]


def _in_mix(x, g, w, vecs, ws, bs, wri, *, seq, bm, gw):
    t, d = x.shape
    dh = vecs.shape[1]
    ng = dh // gw
    n_sub = gw // HEAD_DIM
    n_mat = (t // bm) * ng

    def mat_idx(s):
        sm = jnp.minimum(s, n_mat - 1)
        return sm // ng, lax.rem(sm, ng)

    def mix_idx(s):
        sp = jnp.maximum(s - 1, 0)
        return sp // ng, lax.rem(sp, ng)

    def w_spec(c):
        return pl.BlockSpec((d, gw), lambda s, c=c: (0, c * ng + mat_idx(s)[1]))

    def per_head(shape):
        return pl.BlockSpec((n_sub,) + shape, lambda s: (mix_idx(s)[1], 0, 0))

    y_spec = pl.BlockSpec((bm, gw), lambda s: mix_idx(s))
    q_spec = pl.BlockSpec((bm, HEAD_DIM), lambda s: (mix_idx(s)[0], 0))
    return pl.pallas_call(
        partial(_in_mix_kernel, ng=ng, blocks_per_seq=seq // bm),
        grid=(n_mat + 1,),
        in_specs=[pl.BlockSpec((bm, d), lambda s: (mat_idx(s)[0], 0), pipeline_mode=_SINGLE),
                  pl.BlockSpec((1, d), lambda s: (0, 0)),
                  w_spec(0), w_spec(1), w_spec(2), w_spec(3),
                  pl.BlockSpec((_VEC_ROWS, gw), lambda s: (0, mix_idx(s)[1])),
                  per_head((CHUNK, CHUNK)), per_head((CHUNK, 1)),
                  per_head((HEAD_DIM, 2 * HEAD_DIM))],
        out_specs=[y_spec, y_spec, q_spec, q_spec],
        out_shape=[jax.ShapeDtypeStruct((t, dh), _BF16), jax.ShapeDtypeStruct((t, dh), _BF16),
                   jax.ShapeDtypeStruct((t, HEAD_DIM), _F32), jax.ShapeDtypeStruct((t, HEAD_DIM), _F32)],
        scratch_shapes=[pltpu.VMEM((bm, d), _BF16),
                        pltpu.VMEM((4, bm, gw), _F32),
                        pltpu.VMEM((4, bm, gw), _F32),
                        pltpu.VMEM((bm, HEAD_DIM), _F32),
                        pltpu.VMEM((bm, HEAD_DIM), _F32),
                        pltpu.VMEM((bm + 2 * SUBLANES, gw), _F32),
                        pltpu.VMEM((ng, SUBLANES, gw), _F32),
                        pltpu.VMEM((ng, SUBLANES, gw), _F32)],
        compiler_params=_params(("arbitrary",)),
        name="in_proj_mixers",
    )(x, g, w, w, w, w, vecs, ws, bs, wri)


def _out_proj_kernel(ya_ref, yb_ref, qa_ref, qb_ref, w_ref, x_ref, gpost_ref, gpre_ref,
                     x1_ref, h2_ref, *, bn):
    da = ya_ref.shape[1]
    db = yb_ref.shape[1]
    scale_a = lax.rsqrt(jnp.sum(qa_ref[...], axis=-1, keepdims=True) * (1.0 / da) + EPS)
    scale_b = lax.rsqrt(jnp.sum(qb_ref[...], axis=-1, keepdims=True) * (1.0 / db) + EPS)
    ya = ya_ref[...]
    yb = yb_ref[...]
    for n in range(w_ref.shape[1] // bn):
        cols = slice(n * bn, (n + 1) * bn)
        x1_ref[:, cols] = (
            scale_a * jnp.dot(ya, w_ref[0:da, cols], preferred_element_type=_F32)
            + scale_b * jnp.dot(yb, w_ref[da:da + db, cols], preferred_element_type=_F32))

    def resid_group(chunks):
        y_scales = [_rms_scale(x1_ref[rows, :]) for rows in chunks]
        x1_scales = []
        for rows, scale in zip(chunks, y_scales):
            x1 = x_ref[rows, :] + x1_ref[rows, :] * scale * gpost_ref[...]
            x1_ref[rows, :] = x1
            x1_scales.append(_rms_scale(x1))
        for rows, scale in zip(chunks, x1_scales):
            h2_ref[rows, :] = (x1_ref[rows, :] * scale * gpre_ref[...]).astype(h2_ref.dtype)
    _for_row_groups(x1_ref.shape[0], resid_group)


def _out_proj(ya, yb, qa, qb, w, x, gpost, gpre, *, bm, bn):
    t, da = ya.shape
    db = yb.shape[1]
    d = w.shape[1]
    row = lambda i: (i, 0)
    const = lambda i: (0, 0)
    return pl.pallas_call(
        partial(_out_proj_kernel, bn=bn),
        grid=(t // bm,),
        in_specs=[pl.BlockSpec((bm, da), row),
                  pl.BlockSpec((bm, db), row),
                  pl.BlockSpec((bm, HEAD_DIM), row),
                  pl.BlockSpec((bm, HEAD_DIM), row),
                  pl.BlockSpec((da + db, d), const),
                  pl.BlockSpec((bm, d), row),
                  pl.BlockSpec((1, d), const),
                  pl.BlockSpec((1, d), const)],
        out_specs=[pl.BlockSpec((bm, d), row), pl.BlockSpec((bm, d), row)],
        out_shape=[jax.ShapeDtypeStruct((t, d), _F32), jax.ShapeDtypeStruct((t, d), _BF16)],
        compiler_params=_params(("arbitrary",)),
        name="out_proj_resid",
    )(ya, yb, qa, qb, w, x, gpost, gpre)


def _ffn_kernel(h_ref, x1_ref, wg_ref, wu_ref, wo_ref, gpost_ref, o_ref, *, nf):
    j = pl.program_id(1)

    @pl.when(j == 0)
    def _():
        o_ref[...] = jnp.zeros(o_ref.shape, _F32)

    @pl.when(j < nf)
    def _():
        wg = wg_ref[...].astype(_BF16)
        wu = wu_ref[...].astype(_BF16)
        wo = wo_ref[...].astype(_BF16)
        n_rb = h_ref.shape[0] // FFN_ROWS

        def gate_up(rb):
            h = h_ref[rb * FFN_ROWS:(rb + 1) * FFN_ROWS, :]
            return (jnp.dot(h, wg, preferred_element_type=_F32),
                    jnp.dot(h, wu, preferred_element_type=_F32))

        def down(rb, gate, up):
            act = (jax.nn.silu(gate) * up).astype(_BF16)
            o_ref[rb * FFN_ROWS:(rb + 1) * FFN_ROWS, :] += jnp.dot(act, wo, preferred_element_type=_F32)

        pending = gate_up(0)
        for rb in range(1, n_rb):
            nxt = gate_up(rb)
            down(rb - 1, *pending)
            pending = nxt
        down(n_rb - 1, *pending)

    @pl.when(j >= nf)
    def _():
        slice_rows = x1_ref.shape[0]
        base = (j - nf) * slice_rows

        def resid_group(chunks):
            out_chunks = [pl.ds(pl.multiple_of(base + rows.start, ROW_CHUNK), ROW_CHUNK) for rows in chunks]
            scales = [_rms_scale(o_ref[rows, :]) for rows in out_chunks]
            for rows, out_rows, scale in zip(chunks, out_chunks, scales):
                o_ref[out_rows, :] = x1_ref[rows, :] + o_ref[out_rows, :] * scale * gpost_ref[...]
        _for_row_groups(slice_rows, resid_group)


def _ffn(h2, x1, w_in, w_out, gpost, *, bm, bf, n_epi):
    t, d = h2.shape
    f = w_out.shape[0]
    nf = f // bf
    last = nf - 1
    return pl.pallas_call(
        partial(_ffn_kernel, nf=nf),
        grid=(t // bm, nf + n_epi),
        in_specs=[pl.BlockSpec((bm, d), lambda i, j: (i, 0), pipeline_mode=_SINGLE),
                  pl.BlockSpec((bm // n_epi, d),
                               lambda i, j: (i * n_epi + jnp.clip(j - nf, 0, n_epi - 1), 0)),
                  pl.BlockSpec((d, bf), lambda i, j: (0, jnp.minimum(j, last))),
                  pl.BlockSpec((d, bf), lambda i, j: (0, nf + jnp.minimum(j, last))),
                  pl.BlockSpec((bf, d), lambda i, j: (jnp.minimum(j, last), 0)),
                  pl.BlockSpec((1, d), lambda i, j: (0, 0))],
        out_specs=pl.BlockSpec((bm, d), lambda i, j: (i, 0), pipeline_mode=_SINGLE),
        out_shape=jax.ShapeDtypeStruct((t, d), _F32),
        compiler_params=_params(("arbitrary", "arbitrary")),
        name="swiglu_ffn",
    )(h2, x1, w_in, w_in, w_out, gpost)


def kernel(x, pre_mix_g, w_in, gmlp_v_norm_g, w_spatial, b_spatial, w_conv, b_conv, w_r, b_r, w_i, b_i, lru_lambda, out_norm_a_g, out_norm_b_g, w_out, post_mix_g, pre_ffn_g, w_ffn_in, w_ffn_out, post_ffn_g):
    batch, seq, d = x.shape
    depth = w_in.shape[0]
    xt = x.reshape(batch * seq, d)
    for l in range(depth):
        vec_rows = [gmlp_v_norm_g[l][None], w_conv[l], b_conv[l][None], b_r[l][None], b_i[l][None],
                    lru_lambda[l][None], out_norm_a_g[l][None], out_norm_b_g[l][None]]
        vecs = jnp.concatenate(vec_rows, axis=0)
        vecs = jnp.pad(vecs, ((0, _VEC_ROWS - vecs.shape[0]), (0, 0)))
        wri = jnp.concatenate([w_r[l], w_i[l]], axis=-1).astype(_BF16)
        ya, yb, qa, qb = _in_mix(
            xt, pre_mix_g[l][None], w_in[l].astype(_BF16), vecs, w_spatial[l],
            b_spatial[l][:, :, None], wri, seq=seq, bm=1024, gw=256)
        x1, h2 = _out_proj(ya, yb, qa, qb, w_out[l].astype(_BF16), xt, post_mix_g[l][None],
                           pre_ffn_g[l][None], bm=256, bn=512)
        xt = _ffn(h2, x1, w_ffn_in[l], w_ffn_out[l], post_ffn_g[l][None],
                  bm=1024, bf=256, n_epi=8)
    return xt.reshape(batch, seq, d)
```

```python
import math
from functools import partial

import jax
import jax.numpy as jnp
from jax import lax
from jax.experimental import pallas as pl
from jax.experimental.pallas import tpu as pltpu

EPS = 1e-6
HEAD_DIM = 128
CHUNK = 128
CONV_WIDTH = 4
LRU_C = 8.0
SUBLANES = 8
ROW_CHUNK = 32
GROUP_ROWS = 128
SCAN_ROWS = 128
MAT_ROWS = 256
FFN_ROWS = 256
VMEM_LIMIT_BYTES = 60 * 1024 * 1024

_BF16 = jnp.bfloat16
_F32 = jnp.float32
_LOG2E = math.log2(math.e)
_GELU_A = -2.0 * math.sqrt(2.0 / math.pi) * _LOG2E
_GELU_B = _GELU_A * 0.044715


def _gelu(x):
    return x / (1.0 + jnp.exp2(x * (_GELU_A + _GELU_B * (x * x))))


def _rms_scale(x):
    return lax.rsqrt(jnp.mean(x * x, axis=-1, keepdims=True) + EPS)


def _for_row_groups(n_rows, body):
    def step(i, carry):
        row0 = i * GROUP_ROWS
        body([pl.ds(pl.multiple_of(row0 + c * ROW_CHUNK, ROW_CHUNK), ROW_CHUNK)
              for c in range(GROUP_ROWS // ROW_CHUNK)])
        return carry
    lax.fori_loop(0, n_rows // GROUP_ROWS, step, 0)


def _params(semantics):
    return pltpu.CompilerParams(dimension_semantics=semantics,
                                vmem_limit_bytes=VMEM_LIMIT_BYTES)


_ROW_VG, _ROW_WC, _ROW_BC, _ROW_BR, _ROW_BI, _ROW_LAM, _ROW_GA, _ROW_GB = 0, 1, 5, 6, 7, 8, 9, 10
_VEC_ROWS = 16

_SINGLE = pl.Buffered(1)


def _scan_rows(a, b, carry):
    tm = a.shape[0]
    sub = lax.broadcasted_iota(jnp.int32, (SUBLANES, a.shape[1]), 0)
    hs = []
    for g in range(tm // SUBLANES):
        sl = slice(g * SUBLANES, (g + 1) * SUBLANES)
        ag, bg = a[sl], b[sl]
        d = 1
        while d < SUBLANES:
            keep = sub >= d
            a_sh = jnp.where(keep, pltpu.roll(ag, d, axis=0), 1.0)
            b_sh = jnp.where(keep, pltpu.roll(bg, d, axis=0), 0.0)
            bg = ag * b_sh + bg
            ag = ag * a_sh
            d *= 2
        h = ag * carry + bg
        hs.append(h)
        carry = jnp.broadcast_to(h[SUBLANES - 1:SUBLANES, :], h.shape)
    return jnp.concatenate(hs, axis=0), carry


def _mix_tasks(p_ref, jp, seq_start, vec_ref, ws_ref, bs_ref, wri_ref, ya_ref, yb_ref, qa_ref, qb_ref,
               xe_ref, hist_ref, carry_ref):
    def vec(row, cols):
        return vec_ref[row:row + 1, cols]

    _, bm, gw = p_ref.shape
    hist = SUBLANES
    n_sub = gw // HEAD_DIM
    gmlp_tasks, lru_tasks = [], []

    def gmlp_chunk(c):
        tril = (lax.broadcasted_iota(jnp.int32, (CHUNK, CHUNK), 0)
                >= lax.broadcasted_iota(jnp.int32, (CHUNK, CHUNK), 1))
        rows = slice(c * CHUNK, (c + 1) * CHUNK)
        sq = None
        for h in range(n_sub):
            cols = slice(h * HEAD_DIM, (h + 1) * HEAD_DIM)
            ws = jnp.where(tril, ws_ref[h], 0.0).astype(_BF16)
            gv = _gelu(p_ref[1, rows, cols])
            vn = (gv * _rms_scale(gv) * vec(_ROW_VG, cols)).astype(_BF16)
            mixed = (jnp.dot(ws, vn, preferred_element_type=_F32)
                     + jnp.broadcast_to(bs_ref[h], (CHUNK, HEAD_DIM)))
            ya = _gelu(p_ref[0, rows, cols]) * mixed
            ya_ref[rows, cols] = (ya * vec(_ROW_GA, cols)).astype(ya_ref.dtype)
            sq = ya * ya if sq is None else sq + ya * ya
        qa_ref[rows, :] = sq

    for c in range(bm // CHUNK):
        gmlp_tasks.append(partial(gmlp_chunk, c))

    def lru_setup():
        xe_ref[0:hist, :] = jnp.where(seq_start, jnp.zeros((hist, gw), _F32), hist_ref[jp])
        xe_ref[hist:hist + bm, :] = p_ref[3]

    def lru_subtile(n, r0):
        cols = slice(n * HEAD_DIM, (n + 1) * HEAD_DIM)
        rows = slice(r0, r0 + SCAN_ROWS)
        if r0 == 0:
            carry = jnp.where(seq_start, jnp.zeros((SUBLANES, HEAD_DIM), _F32), carry_ref[jp, :, cols])
        else:
            carry = carry_ref[jp, :, cols]
        neg = -vec(_ROW_LAM, cols)
        softplus = jnp.maximum(neg, 0.0) + jnp.log1p(jnp.exp(-jnp.abs(neg)))
        log2_a_per_r = (-LRU_C * _LOG2E) * softplus
        xc = vec(_ROW_BC, cols)
        for k in range(CONV_WIDTH):
            tap = xe_ref[pl.ds(r0 + hist - (CONV_WIDTH - 1) + k, SCAN_ROWS), cols]
            xc = xc + tap * vec(_ROW_WC + k, cols)
        ri = jnp.dot(xc.astype(_BF16), wri_ref[n], preferred_element_type=_F32)
        r = jax.nn.sigmoid(ri[:, :HEAD_DIM] + vec(_ROW_BR, cols))
        i = jax.nn.sigmoid(ri[:, HEAD_DIM:] + vec(_ROW_BI, cols))
        a = jnp.exp2(r * log2_a_per_r)
        m2 = jnp.maximum(1.0 - a * a, 1e-12)
        bterm = (m2 * lax.rsqrt(m2)) * (i * xc)
        hseq, carry = _scan_rows(a, bterm, carry)
        carry_ref[jp, :, cols] = carry
        yb = hseq * _gelu(p_ref[2, rows, cols])
        yb_ref[rows, cols] = (yb * vec(_ROW_GB, cols)).astype(yb_ref.dtype)
        if n == 0:
            qb_ref[rows, :] = yb * yb
        else:
            qb_ref[rows, :] += yb * yb

    def lru_finish():
        hist_ref[jp] = xe_ref[bm:bm + hist, :]

    for r0 in range(0, bm, SCAN_ROWS):
        for n in range(n_sub):
            lru_tasks.append(partial(lru_subtile, n, r0))
    return gmlp_tasks, lru_setup, lru_tasks, lru_finish


def _in_mix_kernel(x_ref, g_ref, wu_ref, wv_ref, wg_ref, wx_ref,
                   vec_ref, ws_ref, bs_ref, wri_ref, ya_ref, yb_ref, sa_ref, sb_ref,
                   hn_ref, hnext_ref, p_ref, qa_ref, qb_ref, xe_ref, hist_ref, carry_ref,
                   *, ng, blocks_per_seq):
    g = pl.program_id(0)
    n_mat = pl.num_programs(0) - ng - 1
    m = g - ng
    j = lax.rem(jnp.clip(m, 0, n_mat - 1), ng)
    sp = jnp.clip(m - 1, 0, n_mat - 1)
    ip, jp = sp // ng, lax.rem(sp, ng)
    seq_start = lax.rem(ip, blocks_per_seq) == 0
    slice_rows = x_ref.shape[0]
    row0 = lax.rem(jnp.minimum(g, n_mat - 1), ng) * slice_rows

    def norm_chunk(c):
        x = x_ref[c * ROW_CHUNK:(c + 1) * ROW_CHUNK, :]
        rows = pl.ds(pl.multiple_of(row0 + c * ROW_CHUNK, ROW_CHUNK), ROW_CHUNK)
        hnext_ref[rows, :] = (x * _rms_scale(x) * g_ref[...]).astype(_BF16)

    norm_tasks = [partial(norm_chunk, c) for c in range(slice_rows // ROW_CHUNK)]

    @pl.when(g == 0)
    def _():
        p_ref[...] = jnp.zeros(p_ref.shape, _F32)
        hist_ref[...] = jnp.zeros(hist_ref.shape, _F32)
        carry_ref[...] = jnp.zeros(carry_ref.shape, _F32)

    @pl.when(m < 0)
    def _():
        for task in norm_tasks:
            task()

    @pl.when((m >= 0) & (j == 0))
    def _():
        hn_ref[...] = hnext_ref[...]

    @pl.when(m >= 0)
    def _():
        bm = hn_ref.shape[0]
        n_bands = bm // MAT_ROWS
        w_refs = (wu_ref, wv_ref, wg_ref, wx_ref)

        def mat_piece(k, band):
            rows = slice(band * MAT_ROWS, (band + 1) * MAT_ROWS)
            p_ref[k, rows, :] = jnp.dot(hn_ref[rows, :], w_refs[k][...], preferred_element_type=_F32)

        gmlp_tasks, lru_setup, lru_tasks, lru_finish = _mix_tasks(
            p_ref, jp, seq_start, vec_ref, ws_ref, bs_ref, wri_ref, ya_ref, yb_ref, qa_ref, qb_ref,
            xe_ref, hist_ref, carry_ref)
        lru_setup()
        ready = [partial(mat_piece, 3, band) for band in range(n_bands)]
        per_band = lambda tasks, band: tasks[band * len(tasks) // n_bands:(band + 1) * len(tasks) // n_bands]
        for band in range(n_bands):
            for task in (per_band(gmlp_tasks, band) + per_band(lru_tasks, band)
                         + per_band(norm_tasks, band)):
                if ready:
                    ready.pop(0)()
                task()
            ready += [partial(mat_piece, k, band) for k in range(3)]
        for piece in ready:
            piece()
        lru_finish()

    @pl.when((m >= 0) & (jp == 0))
    def _():
        sa_ref[...] = qa_ref[...]
        sb_ref[...] = qb_ref[...]

    @pl.when((m >= 0) & (jp > 0))
    def _():
        sa_ref[...] += qa_ref[...]
        sb_ref[...] += qb_ref[...]


def _in_mix(x, g, w, vecs, ws, bs, wri, *, seq, bm, gw):
    t, d = x.shape
    dh = vecs.shape[1]
    ng = dh // gw
    n_sub = gw // HEAD_DIM
    n_mat = (t // bm) * ng

    def mat_idx(s):
        sm = jnp.clip(s - ng, 0, n_mat - 1)
        return sm // ng, lax.rem(sm, ng)

    def mix_idx(s):
        sp = jnp.clip(s - ng - 1, 0, n_mat - 1)
        return sp // ng, lax.rem(sp, ng)

    def w_spec(c):
        return pl.BlockSpec((d, gw), lambda s, c=c: (0, c * ng + mat_idx(s)[1]))

    def per_head(shape):
        return pl.BlockSpec((n_sub,) + shape, lambda s: (mix_idx(s)[1], 0, 0))

    y_spec = pl.BlockSpec((bm, gw), lambda s: mix_idx(s))
    q_spec = pl.BlockSpec((bm, HEAD_DIM), lambda s: (mix_idx(s)[0], 0))
    return pl.pallas_call(
        partial(_in_mix_kernel, ng=ng, blocks_per_seq=seq // bm),
        grid=(ng + n_mat + 1,),
        in_specs=[pl.BlockSpec((bm // ng, d), lambda s: (jnp.minimum(s, n_mat - 1), 0)),
                  pl.BlockSpec((1, d), lambda s: (0, 0)),
                  w_spec(0), w_spec(1), w_spec(2), w_spec(3),
                  pl.BlockSpec((_VEC_ROWS, gw), lambda s: (0, mix_idx(s)[1])),
                  per_head((CHUNK, CHUNK)), per_head((CHUNK, 1)),
                  per_head((HEAD_DIM, 2 * HEAD_DIM))],
        out_specs=[y_spec, y_spec, q_spec, q_spec],
        out_shape=[jax.ShapeDtypeStruct((t, dh), _BF16), jax.ShapeDtypeStruct((t, dh), _BF16),
                   jax.ShapeDtypeStruct((t, HEAD_DIM), _F32), jax.ShapeDtypeStruct((t, HEAD_DIM), _F32)],
        scratch_shapes=[pltpu.VMEM((bm, d), _BF16),
                        pltpu.VMEM((bm, d), _BF16),
                        pltpu.VMEM((4, bm, gw), _F32),
                        pltpu.VMEM((bm, HEAD_DIM), _F32),
                        pltpu.VMEM((bm, HEAD_DIM), _F32),
                        pltpu.VMEM((bm + 2 * SUBLANES, gw), _F32),
                        pltpu.VMEM((ng, SUBLANES, gw), _F32),
                        pltpu.VMEM((ng, SUBLANES, gw), _F32)],
        compiler_params=_params(("arbitrary",)),
        name="in_proj_mixers",
    )(x, g, w, w, w, w, vecs, ws, bs, wri)


def _out_proj_kernel(ya_ref, yb_ref, qa_ref, qb_ref, w_ref, x_ref, gpost_ref, gpre_ref,
                     x1_ref, h2_ref, *, bn):
    da = ya_ref.shape[1]
    db = yb_ref.shape[1]
    scale_a = lax.rsqrt(jnp.sum(qa_ref[...], axis=-1, keepdims=True) * (1.0 / da) + EPS)
    scale_b = lax.rsqrt(jnp.sum(qb_ref[...], axis=-1, keepdims=True) * (1.0 / db) + EPS)
    ya = ya_ref[...]
    yb = yb_ref[...]
    for n in range(w_ref.shape[1] // bn):
        cols = slice(n * bn, (n + 1) * bn)
        x1_ref[:, cols] = (
            scale_a * jnp.dot(ya, w_ref[0:da, cols], preferred_element_type=_F32)
            + scale_b * jnp.dot(yb, w_ref[da:da + db, cols], preferred_element_type=_F32))

    def resid_group(chunks):
        y_scales = [_rms_scale(x1_ref[rows, :]) for rows in chunks]
        x1_scales = []
        for rows, scale in zip(chunks, y_scales):
            x1 = x_ref[rows, :] + x1_ref[rows, :] * scale * gpost_ref[...]
            x1_ref[rows, :] = x1
            x1_scales.append(_rms_scale(x1))
        for rows, scale in zip(chunks, x1_scales):
            h2_ref[rows, :] = (x1_ref[rows, :] * scale * gpre_ref[...]).astype(h2_ref.dtype)
    _for_row_groups(x1_ref.shape[0], resid_group)


def _out_proj(ya, yb, qa, qb, w, x, gpost, gpre, *, bm, bn):
    t, da = ya.shape
    db = yb.shape[1]
    d = w.shape[1]
    row = lambda i: (i, 0)
    const = lambda i: (0, 0)
    return pl.pallas_call(
        partial(_out_proj_kernel, bn=bn),
        grid=(t // bm,),
        in_specs=[pl.BlockSpec((bm, da), row),
                  pl.BlockSpec((bm, db), row),
                  pl.BlockSpec((bm, HEAD_DIM), row),
                  pl.BlockSpec((bm, HEAD_DIM), row),
                  pl.BlockSpec((da + db, d), const),
                  pl.BlockSpec((bm, d), row),
                  pl.BlockSpec((1, d), const),
                  pl.BlockSpec((1, d), const)],
        out_specs=[pl.BlockSpec((bm, d), row), pl.BlockSpec((bm, d), row)],
        out_shape=[jax.ShapeDtypeStruct((t, d), _F32), jax.ShapeDtypeStruct((t, d), _BF16)],
        compiler_params=_params(("arbitrary",)),
        name="out_proj_resid",
    )(ya, yb, qa, qb, w, x, gpost, gpre)


def _ffn_kernel(h_ref, x1_ref, wg_ref, wu_ref, wo_ref, gpost_ref, o_ref, *, nf):
    j = pl.program_id(1)

    @pl.when(j == 0)
    def _():
        o_ref[...] = jnp.zeros(o_ref.shape, _F32)

    @pl.when(j < nf)
    def _():
        wg = wg_ref[...].astype(_BF16)
        wu = wu_ref[...].astype(_BF16)
        wo = wo_ref[...].astype(_BF16)
        n_rb = h_ref.shape[0] // FFN_ROWS

        def gate_up(rb):
            h = h_ref[rb * FFN_ROWS:(rb + 1) * FFN_ROWS, :]
            return (jnp.dot(h, wg, preferred_element_type=_F32),
                    jnp.dot(h, wu, preferred_element_type=_F32))

        def down(rb, gate, up):
            act = (jax.nn.silu(gate) * up).astype(_BF16)
            o_ref[rb * FFN_ROWS:(rb + 1) * FFN_ROWS, :] += jnp.dot(act, wo, preferred_element_type=_F32)

        pending = gate_up(0)
        for rb in range(1, n_rb):
            nxt = gate_up(rb)
            down(rb - 1, *pending)
            pending = nxt
        down(n_rb - 1, *pending)

    @pl.when(j >= nf)
    def _():
        slice_rows = x1_ref.shape[0]
        base = (j - nf) * slice_rows

        def resid_group(chunks):
            out_chunks = [pl.ds(pl.multiple_of(base + rows.start, ROW_CHUNK), ROW_CHUNK) for rows in chunks]
            scales = [_rms_scale(o_ref[rows, :]) for rows in out_chunks]
            for rows, out_rows, scale in zip(chunks, out_chunks, scales):
                o_ref[out_rows, :] = x1_ref[rows, :] + o_ref[out_rows, :] * scale * gpost_ref[...]
        _for_row_groups(slice_rows, resid_group)


def _ffn(h2, x1, w_in, w_out, gpost, *, bm, bf, n_epi):
    t, d = h2.shape
    f = w_out.shape[0]
    nf = f // bf
    last = nf - 1
    return pl.pallas_call(
        partial(_ffn_kernel, nf=nf),
        grid=(t // bm, nf + n_epi),
        in_specs=[pl.BlockSpec((bm, d), lambda i, j: (i, 0), pipeline_mode=_SINGLE),
                  pl.BlockSpec((bm // n_epi, d),
                               lambda i, j: (i * n_epi + jnp.clip(j - nf, 0, n_epi - 1), 0)),
                  pl.BlockSpec((d, bf), lambda i, j: (0, jnp.minimum(j, last))),
                  pl.BlockSpec((d, bf), lambda i, j: (0, nf + jnp.minimum(j, last))),
                  pl.BlockSpec((bf, d), lambda i, j: (jnp.minimum(j, last), 0)),
                  pl.BlockSpec((1, d), lambda i, j: (0, 0))],
        out_specs=pl.BlockSpec((bm, d), lambda i, j: (i, 0), pipeline_mode=_SINGLE),
        out_shape=jax.ShapeDtypeStruct((t, d), _F32),
        compiler_params=_params(("arbitrary", "arbitrary")),
        name="swiglu_ffn",
    )(h2, x1, w_in, w_in, w_out, gpost)


def kernel(x, pre_mix_g, w_in, gmlp_v_norm_g, w_spatial, b_spatial, w_conv, b_conv, w_r, b_r, w_i, b_i, lru_lambda, out_norm_a_g, out_norm_b_g, w_out, post_mix_g, pre_ffn_g, w_ffn_in, w_ffn_out, post_ffn_g):
    batch, seq, d = x.shape
    depth = w_in.shape[0]
    xt = x.reshape(batch * seq, d)
    for l in range(depth):
        vec_rows = [gmlp_v_norm_g[l][None], w_conv[l], b_conv[l][None], b_r[l][None], b_i[l][None],
                    lru_lambda[l][None], out_norm_a_g[l][None], out_norm_b_g[l][None]]
        vecs = jnp.concatenate(vec_rows, axis=0)
        vecs = jnp.pad(vecs, ((0, _VEC_ROWS - vecs.shape[0]), (0, 0)))
        wri = jnp.concatenate([w_r[l], w_i[l]], axis=-1).astype(_BF16)
        ya, yb, qa, qb = _in_mix(
            xt, pre_mix_g[l][None], w_in[l].astype(_BF16), vecs, w_spatial[l],
            b_spatial[l][:, :, None], wri, seq=seq, bm=1024, gw=256)
        x1, h2 = _out_proj(ya, yb, qa, qb, w_out[l].astype(_BF16), xt, post_mix_g[l][None],
                           pre_ffn_g[l][None], bm=256, bn=512)
        xt = _ffn(h2, x1, w_ffn_in[l], w_ffn_out[l], post_ffn_g[l][None],
                  bm=1024, bf=256, n_epi=8)
    return xt.reshape(batch, seq, d)
```

```python
import math
from functools import partial

import jax
import jax.numpy as jnp
from jax import lax
from jax.experimental import pallas as pl
from jax.experimental.pallas import tpu as pltpu

EPS = 1e-6
HEAD_DIM = 128
CHUNK = 128
CONV_WIDTH = 4
LRU_C = 8.0
SUBLANES = 8
ROW_CHUNK = 32
GROUP_ROWS = 128
SCAN_ROWS = 128
MAT_ROWS = 256
FFN_ROWS = 256
VMEM_LIMIT_BYTES = 60 * 1024 * 1024

_BF16 = jnp.bfloat16
_F32 = jnp.float32
_LOG2E = math.log2(math.e)
_GELU_A = -2.0 * math.sqrt(2.0 / math.pi) * _LOG2E
_GELU_B = _GELU_A * 0.044715


def _gelu(x):
    return x / (1.0 + jnp.exp2(x * (_GELU_A + _GELU_B * (x * x))))


def _rms_scale(x):
    return lax.rsqrt(jnp.mean(x * x, axis=-1, keepdims=True) + EPS)


def _for_row_groups(n_rows, body):
    def step(i, carry):
        row0 = i * GROUP_ROWS
        body([pl.ds(pl.multiple_of(row0 + c * ROW_CHUNK, ROW_CHUNK), ROW_CHUNK)
              for c in range(GROUP_ROWS // ROW_CHUNK)])
        return carry
    lax.fori_loop(0, n_rows // GROUP_ROWS, step, 0)


def _params(semantics):
    return pltpu.CompilerParams(dimension_semantics=semantics,
                                vmem_limit_bytes=VMEM_LIMIT_BYTES)


_ROW_VG, _ROW_WC, _ROW_BC, _ROW_BR, _ROW_BI, _ROW_LAM, _ROW_GA, _ROW_GB = 0, 1, 5, 6, 7, 8, 9, 10
_VEC_ROWS = 16

_SINGLE = pl.Buffered(1)


def _scan_rows(a, b, carry):
    tm = a.shape[0]
    sub = lax.broadcasted_iota(jnp.int32, (SUBLANES, a.shape[1]), 0)
    hs = []
    for g in range(tm // SUBLANES):
        sl = slice(g * SUBLANES, (g + 1) * SUBLANES)
        ag, bg = a[sl], b[sl]
        d = 1
        while d < SUBLANES:
            keep = sub >= d
            a_sh = jnp.where(keep, pltpu.roll(ag, d, axis=0), 1.0)
            b_sh = jnp.where(keep, pltpu.roll(bg, d, axis=0), 0.0)
            bg = ag * b_sh + bg
            ag = ag * a_sh
            d *= 2
        h = ag * carry + bg
        hs.append(h)
        carry = jnp.broadcast_to(h[SUBLANES - 1:SUBLANES, :], h.shape)
    return jnp.concatenate(hs, axis=0), carry


def _mix_tasks(p_ref, jp, seq_start, vec_ref, ws_ref, bs_ref, wri_ref, ya_ref, yb_ref, qa_ref, qb_ref,
               xe_ref, hist_ref, carry_ref):
    def vec(row, cols):
        return vec_ref[row:row + 1, cols]

    _, bm, gw = p_ref.shape
    hist = SUBLANES
    n_sub = gw // HEAD_DIM
    gmlp_tasks, lru_tasks = [], []

    def gmlp_chunk(c):
        tril = (lax.broadcasted_iota(jnp.int32, (CHUNK, CHUNK), 0)
                >= lax.broadcasted_iota(jnp.int32, (CHUNK, CHUNK), 1))
        rows = slice(c * CHUNK, (c + 1) * CHUNK)
        sq = None
        for h in range(n_sub):
            cols = slice(h * HEAD_DIM, (h + 1) * HEAD_DIM)
            ws = jnp.where(tril, ws_ref[h], 0.0).astype(_BF16)
            gv = _gelu(p_ref[1, rows, cols])
            vn = (gv * _rms_scale(gv) * vec(_ROW_VG, cols)).astype(_BF16)
            mixed = (jnp.dot(ws, vn, preferred_element_type=_F32)
                     + jnp.broadcast_to(bs_ref[h], (CHUNK, HEAD_DIM)))
            ya = _gelu(p_ref[0, rows, cols]) * mixed
            ya_ref[rows, cols] = (ya * vec(_ROW_GA, cols)).astype(ya_ref.dtype)
            sq = ya * ya if sq is None else sq + ya * ya
        qa_ref[rows, :] = sq

    for c in range(bm // CHUNK):
        gmlp_tasks.append(partial(gmlp_chunk, c))

    def lru_setup():
        xe_ref[0:hist, :] = jnp.where(seq_start, jnp.zeros((hist, gw), _F32), hist_ref[jp])
        xe_ref[hist:hist + bm, :] = p_ref[3]

    def lru_subtile(n, r0):
        cols = slice(n * HEAD_DIM, (n + 1) * HEAD_DIM)
        rows = slice(r0, r0 + SCAN_ROWS)
        if r0 == 0:
            carry = jnp.where(seq_start, jnp.zeros((SUBLANES, HEAD_DIM), _F32), carry_ref[jp, :, cols])
        else:
            carry = carry_ref[jp, :, cols]
        neg = -vec(_ROW_LAM, cols)
        softplus = jnp.maximum(neg, 0.0) + jnp.log1p(jnp.exp(-jnp.abs(neg)))
        log2_a_per_r = (-LRU_C * _LOG2E) * softplus
        xc = vec(_ROW_BC, cols)
        for k in range(CONV_WIDTH):
            tap = xe_ref[pl.ds(r0 + hist - (CONV_WIDTH - 1) + k, SCAN_ROWS), cols]
            xc = xc + tap * vec(_ROW_WC + k, cols)
        ri = jnp.dot(xc.astype(_BF16), wri_ref[n], preferred_element_type=_F32)
        r = jax.nn.sigmoid(ri[:, :HEAD_DIM] + vec(_ROW_BR, cols))
        i = jax.nn.sigmoid(ri[:, HEAD_DIM:] + vec(_ROW_BI, cols))
        a = jnp.exp2(r * log2_a_per_r)
        m2 = jnp.maximum(1.0 - a * a, 1e-12)
        bterm = (m2 * lax.rsqrt(m2)) * (i * xc)
        hseq, carry = _scan_rows(a, bterm, carry)
        carry_ref[jp, :, cols] = carry
        yb = hseq * _gelu(p_ref[2, rows, cols])
        yb_ref[rows, cols] = (yb * vec(_ROW_GB, cols)).astype(yb_ref.dtype)
        if n == 0:
            qb_ref[rows, :] = yb * yb
        else:
            qb_ref[rows, :] += yb * yb

    def lru_finish():
        hist_ref[jp] = xe_ref[bm:bm + hist, :]

    for r0 in range(0, bm, SCAN_ROWS):
        for n in range(n_sub):
            lru_tasks.append(partial(lru_subtile, n, r0))
    return gmlp_tasks, lru_setup, lru_tasks, lru_finish


def _in_mix_kernel(x_ref, g_ref, wu_ref, wv_ref, wg_ref, wx_ref,
                   vec_ref, ws_ref, bs_ref, wri_ref, wside_ref,
                   ya_ref, yb_ref, sa_ref, sb_ref, wside16_ref,
                   hn_ref, hnext_ref, p_ref, qa_ref, qb_ref, xe_ref, hist_ref, carry_ref,
                   *, ng, blocks_per_seq):
    g = pl.program_id(0)
    n_mat = pl.num_programs(0) - ng - 1
    m = g - ng
    j = lax.rem(jnp.clip(m, 0, n_mat - 1), ng)
    sp = jnp.clip(m - 1, 0, n_mat - 1)
    ip, jp = sp // ng, lax.rem(sp, ng)
    seq_start = lax.rem(ip, blocks_per_seq) == 0
    slice_rows = x_ref.shape[0]
    row0 = lax.rem(jnp.minimum(g, n_mat - 1), ng) * slice_rows

    def norm_chunk(c):
        x = x_ref[c * ROW_CHUNK:(c + 1) * ROW_CHUNK, :]
        rows = pl.ds(pl.multiple_of(row0 + c * ROW_CHUNK, ROW_CHUNK), ROW_CHUNK)
        hnext_ref[rows, :] = (x * _rms_scale(x) * g_ref[...]).astype(_BF16)

    norm_tasks = [partial(norm_chunk, c) for c in range(slice_rows // ROW_CHUNK)]

    def side_cast_chunk(c):
        rows = slice(c * ROW_CHUNK, (c + 1) * ROW_CHUNK)
        wside16_ref[rows, :] = wside_ref[rows, :].astype(wside16_ref.dtype)

    side_tasks = [partial(side_cast_chunk, c) for c in range(wside_ref.shape[0] // ROW_CHUNK)]

    @pl.when(g == 0)
    def _():
        p_ref[...] = jnp.zeros(p_ref.shape, _F32)
        hist_ref[...] = jnp.zeros(hist_ref.shape, _F32)
        carry_ref[...] = jnp.zeros(carry_ref.shape, _F32)

    @pl.when(m < 0)
    def _():
        for task in norm_tasks:
            task()

    @pl.when((m >= 0) & (j == 0))
    def _():
        hn_ref[...] = hnext_ref[...]

    @pl.when(m >= 0)
    def _():
        bm = hn_ref.shape[0]
        n_bands = bm // MAT_ROWS
        w_refs = (wu_ref, wv_ref, wg_ref, wx_ref)

        def mat_piece(k, band):
            rows = slice(band * MAT_ROWS, (band + 1) * MAT_ROWS)
            p_ref[k, rows, :] = jnp.dot(hn_ref[rows, :], w_refs[k][...], preferred_element_type=_F32)

        gmlp_tasks, lru_setup, lru_tasks, lru_finish = _mix_tasks(
            p_ref, jp, seq_start, vec_ref, ws_ref, bs_ref, wri_ref, ya_ref, yb_ref, qa_ref, qb_ref,
            xe_ref, hist_ref, carry_ref)
        lru_setup()
        ready = [partial(mat_piece, 3, band) for band in range(n_bands)]
        per_band = lambda tasks, band: tasks[band * len(tasks) // n_bands:(band + 1) * len(tasks) // n_bands]
        for band in range(n_bands):
            for task in (per_band(gmlp_tasks, band) + per_band(lru_tasks, band)
                         + per_band(norm_tasks, band) + per_band(side_tasks, band)):
                if ready:
                    ready.pop(0)()
                task()
            ready += [partial(mat_piece, k, band) for k in range(3)]
        for piece in ready:
            piece()
        lru_finish()

    @pl.when((m >= 0) & (jp == 0))
    def _():
        sa_ref[...] = qa_ref[...]
        sb_ref[...] = qb_ref[...]

    @pl.when((m >= 0) & (jp > 0))
    def _():
        sa_ref[...] += qa_ref[...]
        sb_ref[...] += qb_ref[...]


def _in_mix(x, g, w, vecs, ws, bs, wri, w_side, *, seq, bm, gw):
    t, d = x.shape
    dh = vecs.shape[1]
    ng = dh // gw
    n_sub = gw // HEAD_DIM
    n_mat = (t // bm) * ng

    def mat_idx(s):
        sm = jnp.clip(s - ng, 0, n_mat - 1)
        return sm // ng, lax.rem(sm, ng)

    def mix_idx(s):
        sp = jnp.clip(s - ng - 1, 0, n_mat - 1)
        return sp // ng, lax.rem(sp, ng)

    def w_spec(c):
        return pl.BlockSpec((d, gw), lambda s, c=c: (0, c * ng + mat_idx(s)[1]))

    def per_head(shape):
        return pl.BlockSpec((n_sub,) + shape, lambda s: (mix_idx(s)[1], 0, 0))

    side_rows = w_side.shape[0] // n_mat
    assert side_rows * n_mat == w_side.shape[0] and side_rows % ROW_CHUNK == 0
    side_spec = pl.BlockSpec((side_rows, w_side.shape[1]), lambda s: (jnp.clip(s - ng, 0, n_mat - 1), 0))
    y_spec = pl.BlockSpec((bm, gw), lambda s: mix_idx(s))
    q_spec = pl.BlockSpec((bm, HEAD_DIM), lambda s: (mix_idx(s)[0], 0))
    return pl.pallas_call(
        partial(_in_mix_kernel, ng=ng, blocks_per_seq=seq // bm),
        grid=(ng + n_mat + 1,),
        in_specs=[pl.BlockSpec((bm // ng, d), lambda s: (jnp.minimum(s, n_mat - 1), 0)),
                  pl.BlockSpec((1, d), lambda s: (0, 0)),
                  w_spec(0), w_spec(1), w_spec(2), w_spec(3),
                  pl.BlockSpec((_VEC_ROWS, gw), lambda s: (0, mix_idx(s)[1])),
                  per_head((CHUNK, CHUNK)), per_head((CHUNK, 1)),
                  per_head((HEAD_DIM, 2 * HEAD_DIM)), side_spec],
        out_specs=[y_spec, y_spec, q_spec, q_spec, side_spec],
        out_shape=[jax.ShapeDtypeStruct((t, dh), _BF16), jax.ShapeDtypeStruct((t, dh), _BF16),
                   jax.ShapeDtypeStruct((t, HEAD_DIM), _F32), jax.ShapeDtypeStruct((t, HEAD_DIM), _F32),
                   jax.ShapeDtypeStruct(w_side.shape, _BF16)],
        scratch_shapes=[pltpu.VMEM((bm, d), _BF16),
                        pltpu.VMEM((bm, d), _BF16),
                        pltpu.VMEM((4, bm, gw), _F32),
                        pltpu.VMEM((bm, HEAD_DIM), _F32),
                        pltpu.VMEM((bm, HEAD_DIM), _F32),
                        pltpu.VMEM((bm + 2 * SUBLANES, gw), _F32),
                        pltpu.VMEM((ng, SUBLANES, gw), _F32),
                        pltpu.VMEM((ng, SUBLANES, gw), _F32)],
        compiler_params=_params(("arbitrary",)),
        name="in_proj_mixers",
    )(x, g, w, w, w, w, vecs, ws, bs, wri, w_side)


def _out_proj_kernel(ya_ref, yb_ref, qa_ref, qb_ref, w_ref, x_ref, gpost_ref, gpre_ref,
                     x1_ref, h2_ref, *, bn):
    da = ya_ref.shape[1]
    db = yb_ref.shape[1]
    scale_a = lax.rsqrt(jnp.sum(qa_ref[...], axis=-1, keepdims=True) * (1.0 / da) + EPS)
    scale_b = lax.rsqrt(jnp.sum(qb_ref[...], axis=-1, keepdims=True) * (1.0 / db) + EPS)
    ya = ya_ref[...]
    yb = yb_ref[...]
    for n in range(w_ref.shape[1] // bn):
        cols = slice(n * bn, (n + 1) * bn)
        x1_ref[:, cols] = (
            scale_a * jnp.dot(ya, w_ref[0:da, cols], preferred_element_type=_F32)
            + scale_b * jnp.dot(yb, w_ref[da:da + db, cols], preferred_element_type=_F32))

    def resid_group(chunks):
        y_scales = [_rms_scale(x1_ref[rows, :]) for rows in chunks]
        x1_scales = []
        for rows, scale in zip(chunks, y_scales):
            x1 = x_ref[rows, :] + x1_ref[rows, :] * scale * gpost_ref[...]
            x1_ref[rows, :] = x1
            x1_scales.append(_rms_scale(x1))
        for rows, scale in zip(chunks, x1_scales):
            h2_ref[rows, :] = (x1_ref[rows, :] * scale * gpre_ref[...]).astype(h2_ref.dtype)
    _for_row_groups(x1_ref.shape[0], resid_group)


def _out_proj(ya, yb, qa, qb, w, x, gpost, gpre, *, bm, bn):
    t, da = ya.shape
    db = yb.shape[1]
    d = w.shape[1]
    row = lambda i: (i, 0)
    const = lambda i: (0, 0)
    return pl.pallas_call(
        partial(_out_proj_kernel, bn=bn),
        grid=(t // bm,),
        in_specs=[pl.BlockSpec((bm, da), row),
                  pl.BlockSpec((bm, db), row),
                  pl.BlockSpec((bm, HEAD_DIM), row),
                  pl.BlockSpec((bm, HEAD_DIM), row),
                  pl.BlockSpec((da + db, d), const),
                  pl.BlockSpec((bm, d), row),
                  pl.BlockSpec((1, d), const),
                  pl.BlockSpec((1, d), const)],
        out_specs=[pl.BlockSpec((bm, d), row), pl.BlockSpec((bm, d), row)],
        out_shape=[jax.ShapeDtypeStruct((t, d), _F32), jax.ShapeDtypeStruct((t, d), _BF16)],
        compiler_params=_params(("arbitrary",)),
        name="out_proj_resid",
    )(ya, yb, qa, qb, w, x, gpost, gpre)


def _ffn_kernel(h_ref, x1_ref, wg_ref, wu_ref, wo_ref, gpost_ref, o_ref, *, nf):
    j = pl.program_id(1)

    @pl.when(j == 0)
    def _():
        o_ref[...] = jnp.zeros(o_ref.shape, _F32)

    @pl.when(j < nf)
    def _():
        wg = wg_ref[...].astype(_BF16)
        wu = wu_ref[...].astype(_BF16)
        wo = wo_ref[...].astype(_BF16)
        n_rb = h_ref.shape[0] // FFN_ROWS

        def gate_up(rb):
            h = h_ref[rb * FFN_ROWS:(rb + 1) * FFN_ROWS, :]
            return (jnp.dot(h, wg, preferred_element_type=_F32),
                    jnp.dot(h, wu, preferred_element_type=_F32))

        def down(rb, gate, up):
            act = (jax.nn.silu(gate) * up).astype(_BF16)
            o_ref[rb * FFN_ROWS:(rb + 1) * FFN_ROWS, :] += jnp.dot(act, wo, preferred_element_type=_F32)

        pending = gate_up(0)
        for rb in range(1, n_rb):
            nxt = gate_up(rb)
            down(rb - 1, *pending)
            pending = nxt
        down(n_rb - 1, *pending)

    @pl.when(j >= nf)
    def _():
        slice_rows = x1_ref.shape[0]
        base = (j - nf) * slice_rows

        def resid_group(chunks):
            out_chunks = [pl.ds(pl.multiple_of(base + rows.start, ROW_CHUNK), ROW_CHUNK) for rows in chunks]
            scales = [_rms_scale(o_ref[rows, :]) for rows in out_chunks]
            for rows, out_rows, scale in zip(chunks, out_chunks, scales):
                o_ref[out_rows, :] = x1_ref[rows, :] + o_ref[out_rows, :] * scale * gpost_ref[...]
        _for_row_groups(slice_rows, resid_group)


def _ffn(h2, x1, w_in, w_out, gpost, *, bm, bf, n_epi):
    t, d = h2.shape
    f = w_out.shape[0]
    nf = f // bf
    last = nf - 1
    return pl.pallas_call(
        partial(_ffn_kernel, nf=nf),
        grid=(t // bm, nf + n_epi),
        in_specs=[pl.BlockSpec((bm, d), lambda i, j: (i, 0), pipeline_mode=_SINGLE),
                  pl.BlockSpec((bm // n_epi, d),
                               lambda i, j: (i * n_epi + jnp.clip(j - nf, 0, n_epi - 1), 0)),
                  pl.BlockSpec((d, bf), lambda i, j: (0, jnp.minimum(j, last))),
                  pl.BlockSpec((d, bf), lambda i, j: (0, nf + jnp.minimum(j, last))),
                  pl.BlockSpec((bf, d), lambda i, j: (jnp.minimum(j, last), 0)),
                  pl.BlockSpec((1, d), lambda i, j: (0, 0))],
        out_specs=pl.BlockSpec((bm, d), lambda i, j: (i, 0), pipeline_mode=_SINGLE),
        out_shape=jax.ShapeDtypeStruct((t, d), _F32),
        compiler_params=_params(("arbitrary", "arbitrary")),
        name="swiglu_ffn",
    )(h2, x1, w_in, w_in, w_out, gpost)


def kernel(x, pre_mix_g, w_in, gmlp_v_norm_g, w_spatial, b_spatial, w_conv, b_conv, w_r, b_r, w_i, b_i, lru_lambda, out_norm_a_g, out_norm_b_g, w_out, post_mix_g, pre_ffn_g, w_ffn_in, w_ffn_out, post_ffn_g):
    batch, seq, d = x.shape
    depth = w_in.shape[0]
    xt = x.reshape(batch * seq, d)
    for l in range(depth):
        vec_rows = [gmlp_v_norm_g[l][None], w_conv[l], b_conv[l][None], b_r[l][None], b_i[l][None],
                    lru_lambda[l][None], out_norm_a_g[l][None], out_norm_b_g[l][None]]
        vecs = jnp.concatenate(vec_rows, axis=0)
        vecs = jnp.pad(vecs, ((0, _VEC_ROWS - vecs.shape[0]), (0, 0)))
        wri = jnp.concatenate([w_r[l], w_i[l]], axis=-1).astype(_BF16)
        ya, yb, qa, qb, w_out16 = _in_mix(
            xt, pre_mix_g[l][None], w_in[l].astype(_BF16), vecs, w_spatial[l],
            b_spatial[l][:, :, None], wri, w_out[l], seq=seq, bm=1024, gw=256)
        x1, h2 = _out_proj(ya, yb, qa, qb, w_out16, xt, post_mix_g[l][None],
                           pre_ffn_g[l][None], bm=256, bn=512)
        xt = _ffn(h2, x1, w_ffn_in[l], w_ffn_out[l], post_ffn_g[l][None],
                  bm=1024, bf=256, n_epi=8)
    return xt.reshape(batch, seq, d)
```

```python
import math
from functools import partial

import jax
import jax.numpy as jnp
from jax import lax
from jax.experimental import pallas as pl
from jax.experimental.pallas import tpu as pltpu

EPS = 1e-6
HEAD_DIM = 128
CHUNK = 128
CONV_WIDTH = 4
LRU_C = 8.0
SUBLANES = 8
ROW_CHUNK = 32
GROUP_ROWS = 128
SCAN_ROWS = 128
MAT_ROWS = 256
FFN_ROWS = 512
VMEM_LIMIT_BYTES = 60 * 1024 * 1024

_BF16 = jnp.bfloat16
_F32 = jnp.float32
_LOG2E = math.log2(math.e)
_GELU_A = -2.0 * math.sqrt(2.0 / math.pi) * _LOG2E
_GELU_B = _GELU_A * 0.044715


def _gelu(x):
    return x / (1.0 + jnp.exp2(x * (_GELU_A + _GELU_B * (x * x))))


def _rms_scale(x):
    return lax.rsqrt(jnp.mean(x * x, axis=-1, keepdims=True) + EPS)


def _for_row_groups(n_rows, body):
    def step(i, carry):
        row0 = i * GROUP_ROWS
        body([pl.ds(pl.multiple_of(row0 + c * ROW_CHUNK, ROW_CHUNK), ROW_CHUNK)
              for c in range(GROUP_ROWS // ROW_CHUNK)])
        return carry
    lax.fori_loop(0, n_rows // GROUP_ROWS, step, 0)


def _params(semantics):
    return pltpu.CompilerParams(dimension_semantics=semantics,
                                vmem_limit_bytes=VMEM_LIMIT_BYTES)


_ROW_VG, _ROW_WC, _ROW_BC, _ROW_BR, _ROW_BI, _ROW_LAM, _ROW_GA, _ROW_GB = 0, 1, 5, 6, 7, 8, 9, 10
_VEC_ROWS = 16

_SINGLE = pl.Buffered(1)


def _scan_rows(a, b, carry):
    tm = a.shape[0]
    sub = lax.broadcasted_iota(jnp.int32, (SUBLANES, a.shape[1]), 0)
    hs = []
    for g in range(tm // SUBLANES):
        sl = slice(g * SUBLANES, (g + 1) * SUBLANES)
        ag, bg = a[sl], b[sl]
        d = 1
        while d < SUBLANES:
            keep = sub >= d
            a_sh = jnp.where(keep, pltpu.roll(ag, d, axis=0), 1.0)
            b_sh = jnp.where(keep, pltpu.roll(bg, d, axis=0), 0.0)
            bg = ag * b_sh + bg
            ag = ag * a_sh
            d *= 2
        h = ag * carry + bg
        hs.append(h)
        carry = jnp.broadcast_to(h[SUBLANES - 1:SUBLANES, :], h.shape)
    return jnp.concatenate(hs, axis=0), carry


def _mix_tasks(p_ref, jp, seq_start, vec_ref, ws_ref, bs_ref, wri_ref, ya_ref, yb_ref, qa_ref, qb_ref,
               xe_ref, hist_ref, carry_ref):
    def vec(row, cols):
        return vec_ref[row:row + 1, cols]

    _, bm, gw = p_ref.shape
    hist = SUBLANES
    n_sub = gw // HEAD_DIM
    gmlp_tasks, lru_tasks = [], []

    def gmlp_chunk(c):
        tril = (lax.broadcasted_iota(jnp.int32, (CHUNK, CHUNK), 0)
                >= lax.broadcasted_iota(jnp.int32, (CHUNK, CHUNK), 1))
        rows = slice(c * CHUNK, (c + 1) * CHUNK)
        sq = None
        for h in range(n_sub):
            cols = slice(h * HEAD_DIM, (h + 1) * HEAD_DIM)
            ws = jnp.where(tril, ws_ref[h], 0.0).astype(_BF16)
            gv = _gelu(p_ref[1, rows, cols])
            vn = (gv * _rms_scale(gv) * vec(_ROW_VG, cols)).astype(_BF16)
            mixed = (jnp.dot(ws, vn, preferred_element_type=_F32)
                     + jnp.broadcast_to(bs_ref[h], (CHUNK, HEAD_DIM)))
            ya = _gelu(p_ref[0, rows, cols]) * mixed
            ya_ref[rows, cols] = (ya * vec(_ROW_GA, cols)).astype(ya_ref.dtype)
            sq = ya * ya if sq is None else sq + ya * ya
        qa_ref[rows, :] = sq

    for c in range(bm // CHUNK):
        gmlp_tasks.append(partial(gmlp_chunk, c))

    def lru_setup():
        xe_ref[0:hist, :] = jnp.where(seq_start, jnp.zeros((hist, gw), _F32), hist_ref[jp])
        xe_ref[hist:hist + bm, :] = p_ref[3]

    def lru_subtile(n, r0):
        cols = slice(n * HEAD_DIM, (n + 1) * HEAD_DIM)
        rows = slice(r0, r0 + SCAN_ROWS)
        if r0 == 0:
            carry = jnp.where(seq_start, jnp.zeros((SUBLANES, HEAD_DIM), _F32), carry_ref[jp, :, cols])
        else:
            carry = carry_ref[jp, :, cols]
        neg = -vec(_ROW_LAM, cols)
        softplus = jnp.maximum(neg, 0.0) + jnp.log1p(jnp.exp(-jnp.abs(neg)))
        log2_a_per_r = (-LRU_C * _LOG2E) * softplus
        xc = vec(_ROW_BC, cols)
        for k in range(CONV_WIDTH):
            tap = xe_ref[pl.ds(r0 + hist - (CONV_WIDTH - 1) + k, SCAN_ROWS), cols]
            xc = xc + tap * vec(_ROW_WC + k, cols)
        ri = jnp.dot(xc.astype(_BF16), wri_ref[n], preferred_element_type=_F32)
        r = jax.nn.sigmoid(ri[:, :HEAD_DIM] + vec(_ROW_BR, cols))
        i = jax.nn.sigmoid(ri[:, HEAD_DIM:] + vec(_ROW_BI, cols))
        a = jnp.exp2(r * log2_a_per_r)
        m2 = jnp.maximum(1.0 - a * a, 1e-12)
        bterm = (m2 * lax.rsqrt(m2)) * (i * xc)
        hseq, carry = _scan_rows(a, bterm, carry)
        carry_ref[jp, :, cols] = carry
        yb = hseq * _gelu(p_ref[2, rows, cols])
        yb_ref[rows, cols] = (yb * vec(_ROW_GB, cols)).astype(yb_ref.dtype)
        if n == 0:
            qb_ref[rows, :] = yb * yb
        else:
            qb_ref[rows, :] += yb * yb

    def lru_finish():
        hist_ref[jp] = xe_ref[bm:bm + hist, :]

    for r0 in range(0, bm, SCAN_ROWS):
        for n in range(n_sub):
            lru_tasks.append(partial(lru_subtile, n, r0))
    return gmlp_tasks, lru_setup, lru_tasks, lru_finish


def _in_mix_kernel(x_ref, g_ref, wu_ref, wv_ref, wg_ref, wx_ref,
                   vec_ref, ws_ref, bs_ref, wri_ref, wside_ref,
                   ya_ref, yb_ref, sa_ref, sb_ref, wside16_ref,
                   hn_ref, hnext_ref, p_ref, qa_ref, qb_ref, xe_ref, hist_ref, carry_ref,
                   *, ng, blocks_per_seq):
    g = pl.program_id(0)
    n_mat = pl.num_programs(0) - ng - 1
    m = g - ng
    j = lax.rem(jnp.clip(m, 0, n_mat - 1), ng)
    sp = jnp.clip(m - 1, 0, n_mat - 1)
    ip, jp = sp // ng, lax.rem(sp, ng)
    seq_start = lax.rem(ip, blocks_per_seq) == 0
    slice_rows = x_ref.shape[0]
    row0 = lax.rem(jnp.minimum(g, n_mat - 1), ng) * slice_rows

    def norm_chunk(c):
        x = x_ref[c * ROW_CHUNK:(c + 1) * ROW_CHUNK, :]
        rows = pl.ds(pl.multiple_of(row0 + c * ROW_CHUNK, ROW_CHUNK), ROW_CHUNK)
        hnext_ref[rows, :] = (x * _rms_scale(x) * g_ref[...]).astype(_BF16)

    norm_tasks = [partial(norm_chunk, c) for c in range(slice_rows // ROW_CHUNK)]

    def side_cast_chunk(c):
        rows = slice(c * ROW_CHUNK, (c + 1) * ROW_CHUNK)
        wside16_ref[rows, :] = wside_ref[rows, :].astype(wside16_ref.dtype)

    side_tasks = [partial(side_cast_chunk, c) for c in range(wside_ref.shape[0] // ROW_CHUNK)]

    @pl.when(g == 0)
    def _():
        p_ref[...] = jnp.zeros(p_ref.shape, _F32)
        hist_ref[...] = jnp.zeros(hist_ref.shape, _F32)
        carry_ref[...] = jnp.zeros(carry_ref.shape, _F32)

    @pl.when(m < 0)
    def _():
        for task in norm_tasks:
            task()

    @pl.when((m >= 0) & (j == 0))
    def _():
        hn_ref[...] = hnext_ref[...]

    @pl.when(m >= 0)
    def _():
        bm = hn_ref.shape[0]
        n_bands = bm // MAT_ROWS
        w_refs = (wu_ref, wv_ref, wg_ref, wx_ref)

        def mat_piece(k, band):
            rows = slice(band * MAT_ROWS, (band + 1) * MAT_ROWS)
            p_ref[k, rows, :] = jnp.dot(hn_ref[rows, :], w_refs[k][...], preferred_element_type=_F32)

        gmlp_tasks, lru_setup, lru_tasks, lru_finish = _mix_tasks(
            p_ref, jp, seq_start, vec_ref, ws_ref, bs_ref, wri_ref, ya_ref, yb_ref, qa_ref, qb_ref,
            xe_ref, hist_ref, carry_ref)
        lru_setup()
        ready = [partial(mat_piece, 3, band) for band in range(n_bands)]
        per_band = lambda tasks, band: tasks[band * len(tasks) // n_bands:(band + 1) * len(tasks) // n_bands]
        for band in range(n_bands):
            for task in (per_band(gmlp_tasks, band) + per_band(lru_tasks, band)
                         + per_band(norm_tasks, band) + per_band(side_tasks, band)):
                if ready:
                    ready.pop(0)()
                task()
            ready += [partial(mat_piece, k, band) for k in range(3)]
        for piece in ready:
            piece()
        lru_finish()

    @pl.when((m >= 0) & (jp == 0))
    def _():
        sa_ref[...] = qa_ref[...]
        sb_ref[...] = qb_ref[...]

    @pl.when((m >= 0) & (jp > 0))
    def _():
        sa_ref[...] += qa_ref[...]
        sb_ref[...] += qb_ref[...]


def _in_mix(x, g, w, vecs, ws, bs, wri, w_side, *, seq, bm, gw):
    t, d = x.shape
    dh = vecs.shape[1]
    ng = dh // gw
    n_sub = gw // HEAD_DIM
    n_mat = (t // bm) * ng

    def mat_idx(s):
        sm = jnp.clip(s - ng, 0, n_mat - 1)
        return sm // ng, lax.rem(sm, ng)

    def mix_idx(s):
        sp = jnp.clip(s - ng - 1, 0, n_mat - 1)
        return sp // ng, lax.rem(sp, ng)

    def w_spec(c):
        return pl.BlockSpec((d, gw), lambda s, c=c: (0, c * ng + mat_idx(s)[1]))

    def per_head(shape):
        return pl.BlockSpec((n_sub,) + shape, lambda s: (mix_idx(s)[1], 0, 0))

    side_rows = w_side.shape[0] // n_mat
    assert side_rows * n_mat == w_side.shape[0] and side_rows % ROW_CHUNK == 0
    side_spec = pl.BlockSpec((side_rows, w_side.shape[1]), lambda s: (jnp.clip(s - ng, 0, n_mat - 1), 0))
    y_spec = pl.BlockSpec((bm, gw), lambda s: mix_idx(s))
    q_spec = pl.BlockSpec((bm, HEAD_DIM), lambda s: (mix_idx(s)[0], 0))
    return pl.pallas_call(
        partial(_in_mix_kernel, ng=ng, blocks_per_seq=seq // bm),
        grid=(ng + n_mat + 1,),
        in_specs=[pl.BlockSpec((bm // ng, d), lambda s: (jnp.minimum(s, n_mat - 1), 0)),
                  pl.BlockSpec((1, d), lambda s: (0, 0)),
                  w_spec(0), w_spec(1), w_spec(2), w_spec(3),
                  pl.BlockSpec((_VEC_ROWS, gw), lambda s: (0, mix_idx(s)[1])),
                  per_head((CHUNK, CHUNK)), per_head((CHUNK, 1)),
                  per_head((HEAD_DIM, 2 * HEAD_DIM)), side_spec],
        out_specs=[y_spec, y_spec, q_spec, q_spec, side_spec],
        out_shape=[jax.ShapeDtypeStruct((t, dh), _BF16), jax.ShapeDtypeStruct((t, dh), _BF16),
                   jax.ShapeDtypeStruct((t, HEAD_DIM), _F32), jax.ShapeDtypeStruct((t, HEAD_DIM), _F32),
                   jax.ShapeDtypeStruct(w_side.shape, _BF16)],
        scratch_shapes=[pltpu.VMEM((bm, d), _BF16),
                        pltpu.VMEM((bm, d), _BF16),
                        pltpu.VMEM((4, bm, gw), _F32),
                        pltpu.VMEM((bm, HEAD_DIM), _F32),
                        pltpu.VMEM((bm, HEAD_DIM), _F32),
                        pltpu.VMEM((bm + 2 * SUBLANES, gw), _F32),
                        pltpu.VMEM((ng, SUBLANES, gw), _F32),
                        pltpu.VMEM((ng, SUBLANES, gw), _F32)],
        compiler_params=_params(("arbitrary",)),
        name="in_proj_mixers",
    )(x, g, w, w, w, w, vecs, ws, bs, wri, w_side)


def _out_proj_kernel(ya_ref, yb_ref, qa_ref, qb_ref, w_ref, x_ref, gpost_ref, gpre_ref,
                     x1_ref, h2_ref, *, bn):
    da = ya_ref.shape[1]
    db = yb_ref.shape[1]
    scale_a = lax.rsqrt(jnp.sum(qa_ref[...], axis=-1, keepdims=True) * (1.0 / da) + EPS)
    scale_b = lax.rsqrt(jnp.sum(qb_ref[...], axis=-1, keepdims=True) * (1.0 / db) + EPS)
    ya = ya_ref[...]
    yb = yb_ref[...]
    for n in range(w_ref.shape[1] // bn):
        cols = slice(n * bn, (n + 1) * bn)
        x1_ref[:, cols] = (
            scale_a * jnp.dot(ya, w_ref[0:da, cols], preferred_element_type=_F32)
            + scale_b * jnp.dot(yb, w_ref[da:da + db, cols], preferred_element_type=_F32))

    def resid_group(chunks):
        y_scales = [_rms_scale(x1_ref[rows, :]) for rows in chunks]
        x1_scales = []
        for rows, scale in zip(chunks, y_scales):
            x1 = x_ref[rows, :] + x1_ref[rows, :] * scale * gpost_ref[...]
            x1_ref[rows, :] = x1
            x1_scales.append(_rms_scale(x1))
        for rows, scale in zip(chunks, x1_scales):
            h2_ref[rows, :] = (x1_ref[rows, :] * scale * gpre_ref[...]).astype(h2_ref.dtype)
    _for_row_groups(x1_ref.shape[0], resid_group)


def _out_proj(ya, yb, qa, qb, w, x, gpost, gpre, *, bm, bn):
    t, da = ya.shape
    db = yb.shape[1]
    d = w.shape[1]
    row = lambda i: (i, 0)
    const = lambda i: (0, 0)
    return pl.pallas_call(
        partial(_out_proj_kernel, bn=bn),
        grid=(t // bm,),
        in_specs=[pl.BlockSpec((bm, da), row),
                  pl.BlockSpec((bm, db), row),
                  pl.BlockSpec((bm, HEAD_DIM), row),
                  pl.BlockSpec((bm, HEAD_DIM), row),
                  pl.BlockSpec((da + db, d), const),
                  pl.BlockSpec((bm, d), row),
                  pl.BlockSpec((1, d), const),
                  pl.BlockSpec((1, d), const)],
        out_specs=[pl.BlockSpec((bm, d), row), pl.BlockSpec((bm, d), row)],
        out_shape=[jax.ShapeDtypeStruct((t, d), _F32), jax.ShapeDtypeStruct((t, d), _BF16)],
        compiler_params=_params(("arbitrary",)),
        name="out_proj_resid",
    )(ya, yb, qa, qb, w, x, gpost, gpre)


def _ffn_kernel(h_ref, x1_ref, wg_ref, wu_ref, wo_ref, gpost_ref, o_ref, *, nf):
    j = pl.program_id(1)

    @pl.when(j == 0)
    def _():
        o_ref[...] = jnp.zeros(o_ref.shape, _F32)

    @pl.when(j < nf)
    def _():
        wg = wg_ref[...].astype(_BF16)
        wu = wu_ref[...].astype(_BF16)
        wo = wo_ref[...].astype(_BF16)
        n_rb = h_ref.shape[0] // FFN_ROWS

        def gate_up(rb):
            h = h_ref[rb * FFN_ROWS:(rb + 1) * FFN_ROWS, :]
            return (jnp.dot(h, wg, preferred_element_type=_F32),
                    jnp.dot(h, wu, preferred_element_type=_F32))

        def down(rb, gate, up):
            act = (jax.nn.silu(gate) * up).astype(_BF16)
            o_ref[rb * FFN_ROWS:(rb + 1) * FFN_ROWS, :] += jnp.dot(act, wo, preferred_element_type=_F32)

        pending = gate_up(0)
        for rb in range(1, n_rb):
            nxt = gate_up(rb)
            down(rb - 1, *pending)
            pending = nxt
        down(n_rb - 1, *pending)

    @pl.when(j >= nf)
    def _():
        slice_rows = x1_ref.shape[0]
        base = (j - nf) * slice_rows

        def resid_group(chunks):
            out_chunks = [pl.ds(pl.multiple_of(base + rows.start, ROW_CHUNK), ROW_CHUNK) for rows in chunks]
            scales = [_rms_scale(o_ref[rows, :]) for rows in out_chunks]
            for rows, out_rows, scale in zip(chunks, out_chunks, scales):
                o_ref[out_rows, :] = x1_ref[rows, :] + o_ref[out_rows, :] * scale * gpost_ref[...]
        _for_row_groups(slice_rows, resid_group)


def _ffn(h2, x1, w_in, w_out, gpost, *, bm, bf, n_epi):
    t, d = h2.shape
    f = w_out.shape[0]
    nf = f // bf
    last = nf - 1
    return pl.pallas_call(
        partial(_ffn_kernel, nf=nf),
        grid=(t // bm, nf + n_epi),
        in_specs=[pl.BlockSpec((bm, d), lambda i, j: (i, 0), pipeline_mode=_SINGLE),
                  pl.BlockSpec((bm // n_epi, d),
                               lambda i, j: (i * n_epi + jnp.clip(j - nf, 0, n_epi - 1), 0)),
                  pl.BlockSpec((d, bf), lambda i, j: (0, jnp.minimum(j, last))),
                  pl.BlockSpec((d, bf), lambda i, j: (0, nf + jnp.minimum(j, last))),
                  pl.BlockSpec((bf, d), lambda i, j: (jnp.minimum(j, last), 0)),
                  pl.BlockSpec((1, d), lambda i, j: (0, 0))],
        out_specs=pl.BlockSpec((bm, d), lambda i, j: (i, 0), pipeline_mode=_SINGLE),
        out_shape=jax.ShapeDtypeStruct((t, d), _F32),
        compiler_params=_params(("arbitrary", "arbitrary")),
        name="swiglu_ffn",
    )(h2, x1, w_in, w_in, w_out, gpost)


def kernel(x, pre_mix_g, w_in, gmlp_v_norm_g, w_spatial, b_spatial, w_conv, b_conv, w_r, b_r, w_i, b_i, lru_lambda, out_norm_a_g, out_norm_b_g, w_out, post_mix_g, pre_ffn_g, w_ffn_in, w_ffn_out, post_ffn_g):
    batch, seq, d = x.shape
    depth = w_in.shape[0]
    xt = x.reshape(batch * seq, d)
    for l in range(depth):
        vec_rows = [gmlp_v_norm_g[l][None], w_conv[l], b_conv[l][None], b_r[l][None], b_i[l][None],
                    lru_lambda[l][None], out_norm_a_g[l][None], out_norm_b_g[l][None]]
        vecs = jnp.concatenate(vec_rows, axis=0)
        vecs = jnp.pad(vecs, ((0, _VEC_ROWS - vecs.shape[0]), (0, 0)))
        wri = jnp.concatenate([w_r[l], w_i[l]], axis=-1).astype(_BF16)
        ya, yb, qa, qb, w_out16 = _in_mix(
            xt, pre_mix_g[l][None], w_in[l].astype(_BF16), vecs, w_spatial[l],
            b_spatial[l][:, :, None], wri, w_out[l], seq=seq, bm=1024, gw=256)
        x1, h2 = _out_proj(ya, yb, qa, qb, w_out16, xt, post_mix_g[l][None],
                           pre_ffn_g[l][None], bm=256, bn=512)
        xt = _ffn(h2, x1, w_ffn_in[l], w_ffn_out[l], post_ffn_g[l][None],
                  bm=1024, bf=256, n_epi=8)
    return xt.reshape(batch, seq, d)
```

```python
import math
from functools import partial

import jax
import jax.numpy as jnp
from jax import lax
from jax.experimental import pallas as pl
from jax.experimental.pallas import tpu as pltpu

EPS = 1e-6
HEAD_DIM = 128
CHUNK = 128
CONV_WIDTH = 4
LRU_C = 8.0
SUBLANES = 8
ROW_CHUNK = 32
GROUP_ROWS = 128
SCAN_ROWS = 128
MAT_ROWS = 256
FFN_ROWS = 512
VMEM_LIMIT_BYTES = 60 * 1024 * 1024

IN_BM, IN_GW = 1024, 256
OUT_BM, OUT_BN = 256, 512
FFN_BM, FFN_BF, FFN_EPI_SLICES = 1024, 256, 8

_BF16 = jnp.bfloat16
_F32 = jnp.float32
_LOG2E = math.log2(math.e)
_GELU_A = -2.0 * math.sqrt(2.0 / math.pi) * _LOG2E
_GELU_B = _GELU_A * 0.044715


def _gelu(x):
    return x / (1.0 + jnp.exp2(x * (_GELU_A + _GELU_B * (x * x))))


def _rms_scale(x):
    return lax.rsqrt(jnp.mean(x * x, axis=-1, keepdims=True) + EPS)


def _for_row_groups(n_rows, body):
    def step(i, carry):
        row0 = i * GROUP_ROWS
        body([pl.ds(pl.multiple_of(row0 + c * ROW_CHUNK, ROW_CHUNK), ROW_CHUNK)
              for c in range(GROUP_ROWS // ROW_CHUNK)])
        return carry
    lax.fori_loop(0, n_rows // GROUP_ROWS, step, 0)


def _params(semantics):
    return pltpu.CompilerParams(dimension_semantics=semantics,
                                vmem_limit_bytes=VMEM_LIMIT_BYTES)


_ROW_VG, _ROW_WC, _ROW_BC, _ROW_BR, _ROW_BI, _ROW_LAM, _ROW_GA, _ROW_GB = 0, 1, 5, 6, 7, 8, 9, 10
_VEC_ROWS = 16

_SINGLE = pl.Buffered(1)


def _scan_rows(a, b, carry):
    tm = a.shape[0]
    sub = lax.broadcasted_iota(jnp.int32, (SUBLANES, a.shape[1]), 0)
    hs = []
    for g in range(tm // SUBLANES):
        sl = slice(g * SUBLANES, (g + 1) * SUBLANES)
        ag, bg = a[sl], b[sl]
        d = 1
        while d < SUBLANES:
            keep = sub >= d
            a_sh = jnp.where(keep, pltpu.roll(ag, d, axis=0), 1.0)
            b_sh = jnp.where(keep, pltpu.roll(bg, d, axis=0), 0.0)
            bg = ag * b_sh + bg
            ag = ag * a_sh
            d *= 2
        h = ag * carry + bg
        hs.append(h)
        carry = jnp.broadcast_to(h[SUBLANES - 1:SUBLANES, :], h.shape)
    return jnp.concatenate(hs, axis=0), carry


def _mix_tasks(p_ref, jp, seq_start, vec_ref, ws_ref, bs_ref, wri_ref, ya_ref, yb_ref, qa_ref, qb_ref,
               xe_ref, hist_ref, carry_ref):
    def vec(row, cols):
        return vec_ref[row:row + 1, cols]

    _, bm, gw = p_ref.shape
    hist = SUBLANES
    n_sub = gw // HEAD_DIM
    gmlp_tasks, lru_tasks = [], []

    def gmlp_chunk(c):
        tril = (lax.broadcasted_iota(jnp.int32, (CHUNK, CHUNK), 0)
                >= lax.broadcasted_iota(jnp.int32, (CHUNK, CHUNK), 1))
        rows = slice(c * CHUNK, (c + 1) * CHUNK)
        sq = None
        for h in range(n_sub):
            cols = slice(h * HEAD_DIM, (h + 1) * HEAD_DIM)
            ws = jnp.where(tril, ws_ref[h], 0.0).astype(_BF16)
            gv = _gelu(p_ref[1, rows, cols])
            vn = (gv * _rms_scale(gv) * vec(_ROW_VG, cols)).astype(_BF16)
            mixed = (jnp.dot(ws, vn, preferred_element_type=_F32)
                     + jnp.broadcast_to(bs_ref[h], (CHUNK, HEAD_DIM)))
            ya = _gelu(p_ref[0, rows, cols]) * mixed
            ya_ref[rows, cols] = (ya * vec(_ROW_GA, cols)).astype(ya_ref.dtype)
            sq = ya * ya if sq is None else sq + ya * ya
        qa_ref[rows, :] = sq

    for c in range(bm // CHUNK):
        gmlp_tasks.append(partial(gmlp_chunk, c))

    def lru_setup():
        xe_ref[0:hist, :] = jnp.where(seq_start, jnp.zeros((hist, gw), _F32), hist_ref[jp])
        xe_ref[hist:hist + bm, :] = p_ref[3]

    def lru_subtile(n, r0):
        cols = slice(n * HEAD_DIM, (n + 1) * HEAD_DIM)
        rows = slice(r0, r0 + SCAN_ROWS)
        if r0 == 0:
            carry = jnp.where(seq_start, jnp.zeros((SUBLANES, HEAD_DIM), _F32), carry_ref[jp, :, cols])
        else:
            carry = carry_ref[jp, :, cols]
        neg = -vec(_ROW_LAM, cols)
        softplus = jnp.maximum(neg, 0.0) + jnp.log1p(jnp.exp(-jnp.abs(neg)))
        log2_a_per_r = (-LRU_C * _LOG2E) * softplus
        xc = vec(_ROW_BC, cols)
        for k in range(CONV_WIDTH):
            tap = xe_ref[pl.ds(r0 + hist - (CONV_WIDTH - 1) + k, SCAN_ROWS), cols]
            xc = xc + tap * vec(_ROW_WC + k, cols)
        ri = jnp.dot(xc.astype(_BF16), wri_ref[n], preferred_element_type=_F32)
        r = jax.nn.sigmoid(ri[:, :HEAD_DIM] + vec(_ROW_BR, cols))
        i = jax.nn.sigmoid(ri[:, HEAD_DIM:] + vec(_ROW_BI, cols))
        a = jnp.exp2(r * log2_a_per_r)
        m2 = jnp.maximum(1.0 - a * a, 1e-12)
        bterm = (m2 * lax.rsqrt(m2)) * (i * xc)
        hseq, carry = _scan_rows(a, bterm, carry)
        carry_ref[jp, :, cols] = carry
        yb = hseq * _gelu(p_ref[2, rows, cols])
        yb_ref[rows, cols] = (yb * vec(_ROW_GB, cols)).astype(yb_ref.dtype)
        if n == 0:
            qb_ref[rows, :] = yb * yb
        else:
            qb_ref[rows, :] += yb * yb

    def lru_finish():
        hist_ref[jp] = xe_ref[bm:bm + hist, :]

    for r0 in range(0, bm, SCAN_ROWS):
        for n in range(n_sub):
            lru_tasks.append(partial(lru_subtile, n, r0))
    return gmlp_tasks, lru_setup, lru_tasks, lru_finish


def _in_mix_kernel(x_ref, g_ref, wu_ref, wv_ref, wg_ref, wx_ref,
                   vec_ref, ws_ref, bs_ref, wri_ref, wside_ref,
                   ya_ref, yb_ref, sa_ref, sb_ref, wside16_ref,
                   hn_ref, hnext_ref, p_ref, qa_ref, qb_ref, xe_ref, hist_ref, carry_ref,
                   *, ng, blocks_per_seq):
    g = pl.program_id(0)
    n_mat = pl.num_programs(0) - ng - 1
    m = g - ng
    j = lax.rem(jnp.clip(m, 0, n_mat - 1), ng)
    sp = jnp.clip(m - 1, 0, n_mat - 1)
    ip, jp = sp // ng, lax.rem(sp, ng)
    seq_start = lax.rem(ip, blocks_per_seq) == 0
    slice_rows = x_ref.shape[0]
    row0 = lax.rem(jnp.minimum(g, n_mat - 1), ng) * slice_rows

    def norm_chunk(c):
        x = x_ref[c * ROW_CHUNK:(c + 1) * ROW_CHUNK, :]
        rows = pl.ds(pl.multiple_of(row0 + c * ROW_CHUNK, ROW_CHUNK), ROW_CHUNK)
        hnext_ref[rows, :] = (x * _rms_scale(x) * g_ref[...]).astype(_BF16)

    norm_tasks = [partial(norm_chunk, c) for c in range(slice_rows // ROW_CHUNK)]

    def side_cast_chunk(c):
        rows = slice(c * ROW_CHUNK, (c + 1) * ROW_CHUNK)
        wside16_ref[rows, :] = wside_ref[rows, :].astype(wside16_ref.dtype)

    side_tasks = [partial(side_cast_chunk, c) for c in range(wside_ref.shape[0] // ROW_CHUNK)]

    @pl.when(g == 0)
    def _():
        p_ref[...] = jnp.zeros(p_ref.shape, _F32)
        hist_ref[...] = jnp.zeros(hist_ref.shape, _F32)
        carry_ref[...] = jnp.zeros(carry_ref.shape, _F32)

    @pl.when(m < 0)
    def _():
        for task in norm_tasks:
            task()

    @pl.when((m >= 0) & (j == 0))
    def _():
        hn_ref[...] = hnext_ref[...]

    @pl.when(m >= 0)
    def _():
        bm = hn_ref.shape[0]
        n_bands = bm // MAT_ROWS
        w_refs = (wu_ref, wv_ref, wg_ref, wx_ref)

        def mat_piece(k, band):
            rows = slice(band * MAT_ROWS, (band + 1) * MAT_ROWS)
            p_ref[k, rows, :] = jnp.dot(hn_ref[rows, :], w_refs[k][...], preferred_element_type=_F32)

        gmlp_tasks, lru_setup, lru_tasks, lru_finish = _mix_tasks(
            p_ref, jp, seq_start, vec_ref, ws_ref, bs_ref, wri_ref, ya_ref, yb_ref, qa_ref, qb_ref,
            xe_ref, hist_ref, carry_ref)
        lru_setup()
        ready = [partial(mat_piece, 3, band) for band in range(n_bands)]
        per_band = lambda tasks, band: tasks[band * len(tasks) // n_bands:(band + 1) * len(tasks) // n_bands]
        for band in range(n_bands):
            for task in (per_band(gmlp_tasks, band) + per_band(lru_tasks, band)
                         + per_band(norm_tasks, band) + per_band(side_tasks, band)):
                if ready:
                    ready.pop(0)()
                task()
            ready += [partial(mat_piece, k, band) for k in range(3)]
        for piece in ready:
            piece()
        lru_finish()

    @pl.when((m >= 0) & (jp == 0))
    def _():
        sa_ref[...] = qa_ref[...]
        sb_ref[...] = qb_ref[...]

    @pl.when((m >= 0) & (jp > 0))
    def _():
        sa_ref[...] += qa_ref[...]
        sb_ref[...] += qb_ref[...]


def _in_mix(x, g, w, vecs, ws, bs, wri, w_side, *, seq, bm, gw):
    t, d = x.shape
    dh = vecs.shape[1]
    ng = dh // gw
    n_sub = gw // HEAD_DIM
    n_mat = (t // bm) * ng
    assert t % bm == 0 and seq % bm == 0 and dh % gw == 0 and gw % HEAD_DIM == 0
    assert bm % MAT_ROWS == 0 and MAT_ROWS % SCAN_ROWS == 0 and MAT_ROWS % CHUNK == 0
    assert bm % ng == 0 and (bm // ng) % ROW_CHUNK == 0 and w.shape == (d, 4 * dh)

    def mat_idx(s):
        sm = jnp.clip(s - ng, 0, n_mat - 1)
        return sm // ng, lax.rem(sm, ng)

    def mix_idx(s):
        sp = jnp.clip(s - ng - 1, 0, n_mat - 1)
        return sp // ng, lax.rem(sp, ng)

    def w_spec(c):
        return pl.BlockSpec((d, gw), lambda s, c=c: (0, c * ng + mat_idx(s)[1]))

    def per_head(shape):
        return pl.BlockSpec((n_sub,) + shape, lambda s: (mix_idx(s)[1], 0, 0))

    side_rows = w_side.shape[0] // n_mat
    assert side_rows * n_mat == w_side.shape[0] and side_rows % ROW_CHUNK == 0
    side_spec = pl.BlockSpec((side_rows, w_side.shape[1]), lambda s: (jnp.clip(s - ng, 0, n_mat - 1), 0))
    y_spec = pl.BlockSpec((bm, gw), lambda s: mix_idx(s))
    q_spec = pl.BlockSpec((bm, HEAD_DIM), lambda s: (mix_idx(s)[0], 0))
    return pl.pallas_call(
        partial(_in_mix_kernel, ng=ng, blocks_per_seq=seq // bm),
        grid=(ng + n_mat + 1,),
        in_specs=[pl.BlockSpec((bm // ng, d), lambda s: (jnp.minimum(s, n_mat - 1), 0)),
                  pl.BlockSpec((1, d), lambda s: (0, 0)),
                  w_spec(0), w_spec(1), w_spec(2), w_spec(3),
                  pl.BlockSpec((_VEC_ROWS, gw), lambda s: (0, mix_idx(s)[1])),
                  per_head((CHUNK, CHUNK)), per_head((CHUNK, 1)),
                  per_head((HEAD_DIM, 2 * HEAD_DIM)), side_spec],
        out_specs=[y_spec, y_spec, q_spec, q_spec, side_spec],
        out_shape=[jax.ShapeDtypeStruct((t, dh), _BF16), jax.ShapeDtypeStruct((t, dh), _BF16),
                   jax.ShapeDtypeStruct((t, HEAD_DIM), _F32), jax.ShapeDtypeStruct((t, HEAD_DIM), _F32),
                   jax.ShapeDtypeStruct(w_side.shape, _BF16)],
        scratch_shapes=[pltpu.VMEM((bm, d), _BF16),
                        pltpu.VMEM((bm, d), _BF16),
                        pltpu.VMEM((4, bm, gw), _F32),
                        pltpu.VMEM((bm, HEAD_DIM), _F32),
                        pltpu.VMEM((bm, HEAD_DIM), _F32),
                        pltpu.VMEM((bm + 2 * SUBLANES, gw), _F32),
                        pltpu.VMEM((ng, SUBLANES, gw), _F32),
                        pltpu.VMEM((ng, SUBLANES, gw), _F32)],
        compiler_params=_params(("arbitrary",)),
        name="in_proj_mixers",
    )(x, g, w, w, w, w, vecs, ws, bs, wri, w_side)


def _out_proj_kernel(ya_ref, yb_ref, qa_ref, qb_ref, w_ref, x_ref, gpost_ref, gpre_ref,
                     x1_ref, h2_ref, *, bn):
    da = ya_ref.shape[1]
    db = yb_ref.shape[1]
    scale_a = lax.rsqrt(jnp.sum(qa_ref[...], axis=-1, keepdims=True) * (1.0 / da) + EPS)
    scale_b = lax.rsqrt(jnp.sum(qb_ref[...], axis=-1, keepdims=True) * (1.0 / db) + EPS)
    ya = ya_ref[...]
    yb = yb_ref[...]
    for n in range(w_ref.shape[1] // bn):
        cols = slice(n * bn, (n + 1) * bn)
        x1_ref[:, cols] = (
            scale_a * jnp.dot(ya, w_ref[0:da, cols], preferred_element_type=_F32)
            + scale_b * jnp.dot(yb, w_ref[da:da + db, cols], preferred_element_type=_F32))

    def resid_group(chunks):
        y_scales = [_rms_scale(x1_ref[rows, :]) for rows in chunks]
        x1_scales = []
        for rows, scale in zip(chunks, y_scales):
            x1 = x_ref[rows, :] + x1_ref[rows, :] * scale * gpost_ref[...]
            x1_ref[rows, :] = x1
            x1_scales.append(_rms_scale(x1))
        for rows, scale in zip(chunks, x1_scales):
            h2_ref[rows, :] = (x1_ref[rows, :] * scale * gpre_ref[...]).astype(h2_ref.dtype)
    _for_row_groups(x1_ref.shape[0], resid_group)


def _out_proj(ya, yb, qa, qb, w, x, gpost, gpre, *, bm, bn):
    t, da = ya.shape
    db = yb.shape[1]
    d = w.shape[1]
    assert t % bm == 0 and bm % GROUP_ROWS == 0 and d % bn == 0 and w.shape[0] == da + db
    row = lambda i: (i, 0)
    const = lambda i: (0, 0)
    return pl.pallas_call(
        partial(_out_proj_kernel, bn=bn),
        grid=(t // bm,),
        in_specs=[pl.BlockSpec((bm, da), row),
                  pl.BlockSpec((bm, db), row),
                  pl.BlockSpec((bm, HEAD_DIM), row),
                  pl.BlockSpec((bm, HEAD_DIM), row),
                  pl.BlockSpec((da + db, d), const),
                  pl.BlockSpec((bm, d), row),
                  pl.BlockSpec((1, d), const),
                  pl.BlockSpec((1, d), const)],
        out_specs=[pl.BlockSpec((bm, d), row), pl.BlockSpec((bm, d), row)],
        out_shape=[jax.ShapeDtypeStruct((t, d), _F32), jax.ShapeDtypeStruct((t, d), _BF16)],
        compiler_params=_params(("arbitrary",)),
        name="out_proj_resid",
    )(ya, yb, qa, qb, w, x, gpost, gpre)


def _ffn_kernel(h_ref, x1_ref, wg_ref, wu_ref, wo_ref, gpost_ref, o_ref, *, nf):
    j = pl.program_id(1)

    @pl.when(j == 0)
    def _():
        o_ref[...] = jnp.zeros(o_ref.shape, _F32)

    @pl.when(j < nf)
    def _():
        wg = wg_ref[...].astype(_BF16)
        wu = wu_ref[...].astype(_BF16)
        wo = wo_ref[...].astype(_BF16)
        n_rb = h_ref.shape[0] // FFN_ROWS

        def gate_up(rb):
            h = h_ref[rb * FFN_ROWS:(rb + 1) * FFN_ROWS, :]
            return (jnp.dot(h, wg, preferred_element_type=_F32),
                    jnp.dot(h, wu, preferred_element_type=_F32))

        def down(rb, gate, up):
            act = (jax.nn.silu(gate) * up).astype(_BF16)
            o_ref[rb * FFN_ROWS:(rb + 1) * FFN_ROWS, :] += jnp.dot(act, wo, preferred_element_type=_F32)

        pending = gate_up(0)
        for rb in range(1, n_rb):
            nxt = gate_up(rb)
            down(rb - 1, *pending)
            pending = nxt
        down(n_rb - 1, *pending)

    @pl.when(j >= nf)
    def _():
        slice_rows = x1_ref.shape[0]
        base = (j - nf) * slice_rows

        def resid_group(chunks):
            out_chunks = [pl.ds(pl.multiple_of(base + rows.start, ROW_CHUNK), ROW_CHUNK) for rows in chunks]
            scales = [_rms_scale(o_ref[rows, :]) for rows in out_chunks]
            for rows, out_rows, scale in zip(chunks, out_chunks, scales):
                o_ref[out_rows, :] = x1_ref[rows, :] + o_ref[out_rows, :] * scale * gpost_ref[...]
        _for_row_groups(slice_rows, resid_group)


def _ffn(h2, x1, w_in, w_out, gpost, *, bm, bf, n_epi):
    t, d = h2.shape
    f = w_out.shape[0]
    nf = f // bf
    assert t % bm == 0 and f % bf == 0 and w_in.shape == (d, 2 * f) and bm % FFN_ROWS == 0
    assert bm % n_epi == 0 and (bm // n_epi) % GROUP_ROWS == 0
    last = nf - 1
    return pl.pallas_call(
        partial(_ffn_kernel, nf=nf),
        grid=(t // bm, nf + n_epi),
        in_specs=[pl.BlockSpec((bm, d), lambda i, j: (i, 0), pipeline_mode=_SINGLE),
                  pl.BlockSpec((bm // n_epi, d),
                               lambda i, j: (i * n_epi + jnp.clip(j - nf, 0, n_epi - 1), 0)),
                  pl.BlockSpec((d, bf), lambda i, j: (0, jnp.minimum(j, last))),
                  pl.BlockSpec((d, bf), lambda i, j: (0, nf + jnp.minimum(j, last))),
                  pl.BlockSpec((bf, d), lambda i, j: (jnp.minimum(j, last), 0)),
                  pl.BlockSpec((1, d), lambda i, j: (0, 0))],
        out_specs=pl.BlockSpec((bm, d), lambda i, j: (i, 0), pipeline_mode=_SINGLE),
        out_shape=jax.ShapeDtypeStruct((t, d), _F32),
        compiler_params=_params(("arbitrary", "arbitrary")),
        name="swiglu_ffn",
    )(h2, x1, w_in, w_in, w_out, gpost)


def kernel(x, pre_mix_g, w_in, gmlp_v_norm_g, w_spatial, b_spatial, w_conv, b_conv, w_r, b_r, w_i, b_i, lru_lambda, out_norm_a_g, out_norm_b_g, w_out, post_mix_g, pre_ffn_g, w_ffn_in, w_ffn_out, post_ffn_g):
    batch, seq, d = x.shape
    depth = w_in.shape[0]
    xt = x.reshape(batch * seq, d)
    for l in range(depth):
        vec_rows = [gmlp_v_norm_g[l][None], w_conv[l], b_conv[l][None], b_r[l][None], b_i[l][None],
                    lru_lambda[l][None], out_norm_a_g[l][None], out_norm_b_g[l][None]]
        vecs = jnp.concatenate(vec_rows, axis=0)
        vecs = jnp.pad(vecs, ((0, _VEC_ROWS - vecs.shape[0]), (0, 0)))
        wri = jnp.concatenate([w_r[l], w_i[l]], axis=-1).astype(_BF16)
        ya, yb, qa, qb, w_out16 = _in_mix(
            xt, pre_mix_g[l][None], w_in[l].astype(_BF16), vecs, w_spatial[l],
            b_spatial[l][:, :, None], wri, w_out[l], seq=seq, bm=IN_BM, gw=IN_GW)
        x1, h2 = _out_proj(ya, yb, qa, qb, w_out16, xt, post_mix_g[l][None],
                           pre_ffn_g[l][None], bm=OUT_BM, bn=OUT_BN)
        xt = _ffn(h2, x1, w_ffn_in[l], w_ffn_out[l], post_ffn_g[l][None],
                  bm=FFN_BM, bf=FFN_BF, n_epi=FFN_EPI_SLICES)
    return xt.reshape(batch, seq, d)
```

```python
import math
from functools import partial

import jax
import jax.numpy as jnp
from jax import lax
from jax.experimental import pallas as pl
from jax.experimental.pallas import tpu as pltpu

EPS = 1e-6
HEAD_DIM = 128
CHUNK = 128
CONV_WIDTH = 4
LRU_C = 8.0
SUBLANES = 8
ROW_CHUNK = 32
GROUP_ROWS = 128
SCAN_ROWS = 128
MAT_ROWS = 256
FFN_ROWS = 512
VMEM_LIMIT_BYTES = 60 * 1024 * 1024

IN_BM, IN_GW = 1024, 256
OUT_BM, OUT_BN = 256, 512
FFN_BM, FFN_BF, FFN_EPI_SLICES = 1024, 256, 8

_BF16 = jnp.bfloat16
_F32 = jnp.float32
_LOG2E = math.log2(math.e)
_GELU_A = -2.0 * math.sqrt(2.0 / math.pi) * _LOG2E
_GELU_B = _GELU_A * 0.044715


def _gelu(x):
    return x / (1.0 + jnp.exp2(x * (_GELU_A + _GELU_B * (x * x))))


def _rms_scale(x):
    return lax.rsqrt(jnp.mean(x * x, axis=-1, keepdims=True) + EPS)


def _for_row_groups(n_rows, body):
    def step(i, carry):
        row0 = i * GROUP_ROWS
        body([pl.ds(pl.multiple_of(row0 + c * ROW_CHUNK, ROW_CHUNK), ROW_CHUNK)
              for c in range(GROUP_ROWS // ROW_CHUNK)])
        return carry
    lax.fori_loop(0, n_rows // GROUP_ROWS, step, 0)


def _params(semantics):
    return pltpu.CompilerParams(dimension_semantics=semantics,
                                vmem_limit_bytes=VMEM_LIMIT_BYTES)


_ROW_VG, _ROW_WC, _ROW_BC, _ROW_BR, _ROW_BI, _ROW_LAM, _ROW_GA, _ROW_GB = 0, 1, 5, 6, 7, 8, 9, 10
_VEC_ROWS = 16

_SINGLE = pl.Buffered(1)


def _scan_rows(a, b, carry):
    tm = a.shape[0]
    sub = lax.broadcasted_iota(jnp.int32, (SUBLANES, a.shape[1]), 0)
    hs = []
    for g in range(tm // SUBLANES):
        sl = slice(g * SUBLANES, (g + 1) * SUBLANES)
        ag, bg = a[sl], b[sl]
        d = 1
        while d < SUBLANES:
            keep = sub >= d
            a_sh = jnp.where(keep, pltpu.roll(ag, d, axis=0), 1.0)
            b_sh = jnp.where(keep, pltpu.roll(bg, d, axis=0), 0.0)
            bg = ag * b_sh + bg
            ag = ag * a_sh
            d *= 2
        h = ag * carry + bg
        hs.append(h)
        carry = jnp.broadcast_to(h[SUBLANES - 1:SUBLANES, :], h.shape)
    return jnp.concatenate(hs, axis=0), carry


def _mix_tasks(p_ref, jp, seq_start, vec_ref, ws_ref, bs_ref, wri_ref, ya_ref, yb_ref, qa_ref, qb_ref,
               xe_ref, hist_ref, carry_ref):
    def vec(row, cols):
        return vec_ref[row:row + 1, cols]

    _, bm, gw = p_ref.shape
    hist = SUBLANES
    n_sub = gw // HEAD_DIM
    gmlp_tasks, lru_tasks = [], []

    def gmlp_chunk_pair(c):
        tril = (lax.broadcasted_iota(jnp.int32, (CHUNK, CHUNK), 0)
                >= lax.broadcasted_iota(jnp.int32, (CHUNK, CHUNK), 1))
        row_pair = [slice((c + i) * CHUNK, (c + i + 1) * CHUNK) for i in range(2)]
        sq = [None, None]
        for h in range(n_sub):
            cols = slice(h * HEAD_DIM, (h + 1) * HEAD_DIM)
            ws = jnp.where(tril, ws_ref[h], 0.0).astype(_BF16)
            vns = []
            for rows in row_pair:
                gv = _gelu(p_ref[1, rows, cols])
                vns.append((gv * _rms_scale(gv) * vec(_ROW_VG, cols)).astype(_BF16))
            mixed2 = jnp.dot(ws, jnp.concatenate(vns, axis=-1), preferred_element_type=_F32)
            bias = jnp.broadcast_to(bs_ref[h], (CHUNK, HEAD_DIM))
            for i, rows in enumerate(row_pair):
                mixed = mixed2[:, i * HEAD_DIM:(i + 1) * HEAD_DIM] + bias
                ya = _gelu(p_ref[0, rows, cols]) * mixed
                ya_ref[rows, cols] = (ya * vec(_ROW_GA, cols)).astype(ya_ref.dtype)
                sq[i] = ya * ya if sq[i] is None else sq[i] + ya * ya
        for i, rows in enumerate(row_pair):
            qa_ref[rows, :] = sq[i]

    for c in range(0, bm // CHUNK, 2):
        gmlp_tasks.append(partial(gmlp_chunk_pair, c))

    def lru_setup():
        xe_ref[0:hist, :] = jnp.where(seq_start, jnp.zeros((hist, gw), _F32), hist_ref[jp])
        xe_ref[hist:hist + bm, :] = p_ref[3]

    def lru_subtile(n, r0):
        cols = slice(n * HEAD_DIM, (n + 1) * HEAD_DIM)
        rows = slice(r0, r0 + SCAN_ROWS)
        if r0 == 0:
            carry = jnp.where(seq_start, jnp.zeros((SUBLANES, HEAD_DIM), _F32), carry_ref[jp, :, cols])
        else:
            carry = carry_ref[jp, :, cols]
        neg = -vec(_ROW_LAM, cols)
        softplus = jnp.maximum(neg, 0.0) + jnp.log1p(jnp.exp(-jnp.abs(neg)))
        log2_a_per_r = (-LRU_C * _LOG2E) * softplus
        xc = vec(_ROW_BC, cols)
        for k in range(CONV_WIDTH):
            tap = xe_ref[pl.ds(r0 + hist - (CONV_WIDTH - 1) + k, SCAN_ROWS), cols]
            xc = xc + tap * vec(_ROW_WC + k, cols)
        ri = jnp.dot(xc.astype(_BF16), wri_ref[n], preferred_element_type=_F32)
        r = jax.nn.sigmoid(ri[:, :HEAD_DIM] + vec(_ROW_BR, cols))
        i = jax.nn.sigmoid(ri[:, HEAD_DIM:] + vec(_ROW_BI, cols))
        a = jnp.exp2(r * log2_a_per_r)
        m2 = jnp.maximum(1.0 - a * a, 1e-12)
        bterm = (m2 * lax.rsqrt(m2)) * (i * xc)
        hseq, carry = _scan_rows(a, bterm, carry)
        carry_ref[jp, :, cols] = carry
        yb = hseq * _gelu(p_ref[2, rows, cols])
        yb_ref[rows, cols] = (yb * vec(_ROW_GB, cols)).astype(yb_ref.dtype)
        if n == 0:
            qb_ref[rows, :] = yb * yb
        else:
            qb_ref[rows, :] += yb * yb

    def lru_finish():
        hist_ref[jp] = xe_ref[bm:bm + hist, :]

    for r0 in range(0, bm, SCAN_ROWS):
        for n in range(n_sub):
            lru_tasks.append(partial(lru_subtile, n, r0))
    return gmlp_tasks, lru_setup, lru_tasks, lru_finish


def _in_mix_kernel(x_ref, g_ref, wu_ref, wv_ref, wg_ref, wx_ref,
                   vec_ref, ws_ref, bs_ref, wri_ref, wside_ref,
                   ya_ref, yb_ref, sa_ref, sb_ref, wside16_ref,
                   hn_ref, hnext_ref, p_ref, qa_ref, qb_ref, xe_ref, hist_ref, carry_ref,
                   *, ng, blocks_per_seq):
    g = pl.program_id(0)
    n_mat = pl.num_programs(0) - ng - 1
    m = g - ng
    j = lax.rem(jnp.clip(m, 0, n_mat - 1), ng)
    sp = jnp.clip(m - 1, 0, n_mat - 1)
    ip, jp = sp // ng, lax.rem(sp, ng)
    seq_start = lax.rem(ip, blocks_per_seq) == 0
    slice_rows = x_ref.shape[0]
    row0 = lax.rem(jnp.minimum(g, n_mat - 1), ng) * slice_rows

    def norm_chunk(c):
        x = x_ref[c * ROW_CHUNK:(c + 1) * ROW_CHUNK, :]
        rows = pl.ds(pl.multiple_of(row0 + c * ROW_CHUNK, ROW_CHUNK), ROW_CHUNK)
        hnext_ref[rows, :] = (x * _rms_scale(x) * g_ref[...]).astype(_BF16)

    norm_tasks = [partial(norm_chunk, c) for c in range(slice_rows // ROW_CHUNK)]

    def side_cast_chunk(c):
        rows = slice(c * ROW_CHUNK, (c + 1) * ROW_CHUNK)
        wside16_ref[rows, :] = wside_ref[rows, :].astype(wside16_ref.dtype)

    side_tasks = [partial(side_cast_chunk, c) for c in range(wside_ref.shape[0] // ROW_CHUNK)]

    @pl.when(g == 0)
    def _():
        p_ref[...] = jnp.zeros(p_ref.shape, _F32)
        hist_ref[...] = jnp.zeros(hist_ref.shape, _F32)
        carry_ref[...] = jnp.zeros(carry_ref.shape, _F32)

    @pl.when(m < 0)
    def _():
        for task in norm_tasks:
            task()

    @pl.when((m >= 0) & (j == 0))
    def _():
        hn_ref[...] = hnext_ref[...]

    @pl.when(m >= 0)
    def _():
        bm = hn_ref.shape[0]
        n_bands = bm // MAT_ROWS
        w_refs = (wu_ref, wv_ref, wg_ref, wx_ref)

        def mat_piece(k, band):
            rows = slice(band * MAT_ROWS, (band + 1) * MAT_ROWS)
            p_ref[k, rows, :] = jnp.dot(hn_ref[rows, :], w_refs[k][...], preferred_element_type=_F32)

        gmlp_tasks, lru_setup, lru_tasks, lru_finish = _mix_tasks(
            p_ref, jp, seq_start, vec_ref, ws_ref, bs_ref, wri_ref, ya_ref, yb_ref, qa_ref, qb_ref,
            xe_ref, hist_ref, carry_ref)
        lru_setup()
        ready = [partial(mat_piece, 3, band) for band in range(n_bands)]
        per_band = lambda tasks, band: tasks[band * len(tasks) // n_bands:(band + 1) * len(tasks) // n_bands]
        for band in range(n_bands):
            for task in (per_band(gmlp_tasks, band) + per_band(lru_tasks, band)
                         + per_band(norm_tasks, band) + per_band(side_tasks, band)):
                if ready:
                    ready.pop(0)()
                task()
            ready += [partial(mat_piece, k, band) for k in range(3)]
        for piece in ready:
            piece()
        lru_finish()

    @pl.when((m >= 0) & (jp == 0))
    def _():
        sa_ref[...] = qa_ref[...]
        sb_ref[...] = qb_ref[...]

    @pl.when((m >= 0) & (jp > 0))
    def _():
        sa_ref[...] += qa_ref[...]
        sb_ref[...] += qb_ref[...]


def _in_mix(x, g, w, vecs, ws, bs, wri, w_side, *, seq, bm, gw):
    t, d = x.shape
    dh = vecs.shape[1]
    ng = dh // gw
    n_sub = gw // HEAD_DIM
    n_mat = (t // bm) * ng
    assert t % bm == 0 and seq % bm == 0 and dh % gw == 0 and gw % HEAD_DIM == 0
    assert bm % MAT_ROWS == 0 and MAT_ROWS % SCAN_ROWS == 0 and MAT_ROWS % CHUNK == 0
    assert bm % ng == 0 and (bm // ng) % ROW_CHUNK == 0 and w.shape == (d, 4 * dh)

    def mat_idx(s):
        sm = jnp.clip(s - ng, 0, n_mat - 1)
        return sm // ng, lax.rem(sm, ng)

    def mix_idx(s):
        sp = jnp.clip(s - ng - 1, 0, n_mat - 1)
        return sp // ng, lax.rem(sp, ng)

    def w_spec(c):
        return pl.BlockSpec((d, gw), lambda s, c=c: (0, c * ng + mat_idx(s)[1]))

    def per_head(shape):
        return pl.BlockSpec((n_sub,) + shape, lambda s: (mix_idx(s)[1], 0, 0))

    side_rows = w_side.shape[0] // n_mat
    assert side_rows * n_mat == w_side.shape[0] and side_rows % ROW_CHUNK == 0
    side_spec = pl.BlockSpec((side_rows, w_side.shape[1]), lambda s: (jnp.clip(s - ng, 0, n_mat - 1), 0))
    y_spec = pl.BlockSpec((bm, gw), lambda s: mix_idx(s))
    q_spec = pl.BlockSpec((bm, HEAD_DIM), lambda s: (mix_idx(s)[0], 0))
    return pl.pallas_call(
        partial(_in_mix_kernel, ng=ng, blocks_per_seq=seq // bm),
        grid=(ng + n_mat + 1,),
        in_specs=[pl.BlockSpec((bm // ng, d), lambda s: (jnp.minimum(s, n_mat - 1), 0)),
                  pl.BlockSpec((1, d), lambda s: (0, 0)),
                  w_spec(0), w_spec(1), w_spec(2), w_spec(3),
                  pl.BlockSpec((_VEC_ROWS, gw), lambda s: (0, mix_idx(s)[1])),
                  per_head((CHUNK, CHUNK)), per_head((CHUNK, 1)),
                  per_head((HEAD_DIM, 2 * HEAD_DIM)), side_spec],
        out_specs=[y_spec, y_spec, q_spec, q_spec, side_spec],
        out_shape=[jax.ShapeDtypeStruct((t, dh), _BF16), jax.ShapeDtypeStruct((t, dh), _BF16),
                   jax.ShapeDtypeStruct((t, HEAD_DIM), _F32), jax.ShapeDtypeStruct((t, HEAD_DIM), _F32),
                   jax.ShapeDtypeStruct(w_side.shape, _BF16)],
        scratch_shapes=[pltpu.VMEM((bm, d), _BF16),
                        pltpu.VMEM((bm, d), _BF16),
                        pltpu.VMEM((4, bm, gw), _F32),
                        pltpu.VMEM((bm, HEAD_DIM), _F32),
                        pltpu.VMEM((bm, HEAD_DIM), _F32),
                        pltpu.VMEM((bm + 2 * SUBLANES, gw), _F32),
                        pltpu.VMEM((ng, SUBLANES, gw), _F32),
                        pltpu.VMEM((ng, SUBLANES, gw), _F32)],
        compiler_params=_params(("arbitrary",)),
        name="in_proj_mixers",
    )(x, g, w, w, w, w, vecs, ws, bs, wri, w_side)


def _out_proj_kernel(ya_ref, yb_ref, qa_ref, qb_ref, w_ref, x_ref, gpost_ref, gpre_ref,
                     x1_ref, h2_ref, *, bn):
    da = ya_ref.shape[1]
    db = yb_ref.shape[1]
    scale_a = lax.rsqrt(jnp.sum(qa_ref[...], axis=-1, keepdims=True) * (1.0 / da) + EPS)
    scale_b = lax.rsqrt(jnp.sum(qb_ref[...], axis=-1, keepdims=True) * (1.0 / db) + EPS)
    ya = ya_ref[...]
    yb = yb_ref[...]
    for n in range(w_ref.shape[1] // bn):
        cols = slice(n * bn, (n + 1) * bn)
        x1_ref[:, cols] = (
            scale_a * jnp.dot(ya, w_ref[0:da, cols], preferred_element_type=_F32)
            + scale_b * jnp.dot(yb, w_ref[da:da + db, cols], preferred_element_type=_F32))

    def resid_group(chunks):
        y_scales = [_rms_scale(x1_ref[rows, :]) for rows in chunks]
        x1_scales = []
        for rows, scale in zip(chunks, y_scales):
            x1 = x_ref[rows, :] + x1_ref[rows, :] * scale * gpost_ref[...]
            x1_ref[rows, :] = x1
            x1_scales.append(_rms_scale(x1))
        for rows, scale in zip(chunks, x1_scales):
            h2_ref[rows, :] = (x1_ref[rows, :] * scale * gpre_ref[...]).astype(h2_ref.dtype)
    _for_row_groups(x1_ref.shape[0], resid_group)


def _out_proj(ya, yb, qa, qb, w, x, gpost, gpre, *, bm, bn):
    t, da = ya.shape
    db = yb.shape[1]
    d = w.shape[1]
    assert t % bm == 0 and bm % GROUP_ROWS == 0 and d % bn == 0 and w.shape[0] == da + db
    row = lambda i: (i, 0)
    const = lambda i: (0, 0)
    return pl.pallas_call(
        partial(_out_proj_kernel, bn=bn),
        grid=(t // bm,),
        in_specs=[pl.BlockSpec((bm, da), row),
                  pl.BlockSpec((bm, db), row),
                  pl.BlockSpec((bm, HEAD_DIM), row),
                  pl.BlockSpec((bm, HEAD_DIM), row),
                  pl.BlockSpec((da + db, d), const),
                  pl.BlockSpec((bm, d), row),
                  pl.BlockSpec((1, d), const),
                  pl.BlockSpec((1, d), const)],
        out_specs=[pl.BlockSpec((bm, d), row), pl.BlockSpec((bm, d), row)],
        out_shape=[jax.ShapeDtypeStruct((t, d), _F32), jax.ShapeDtypeStruct((t, d), _BF16)],
        compiler_params=_params(("arbitrary",)),
        name="out_proj_resid",
    )(ya, yb, qa, qb, w, x, gpost, gpre)


def _ffn_kernel(h_ref, x1_ref, wg_ref, wu_ref, wo_ref, gpost_ref, o_ref, *, nf):
    j = pl.program_id(1)

    @pl.when(j == 0)
    def _():
        o_ref[...] = jnp.zeros(o_ref.shape, _F32)

    @pl.when(j < nf)
    def _():
        wg = wg_ref[...].astype(_BF16)
        wu = wu_ref[...].astype(_BF16)
        wo = wo_ref[...].astype(_BF16)
        n_rb = h_ref.shape[0] // FFN_ROWS

        def gate_up(rb):
            h = h_ref[rb * FFN_ROWS:(rb + 1) * FFN_ROWS, :]
            return (jnp.dot(h, wg, preferred_element_type=_F32),
                    jnp.dot(h, wu, preferred_element_type=_F32))

        def down(rb, gate, up):
            act = (jax.nn.silu(gate) * up).astype(_BF16)
            o_ref[rb * FFN_ROWS:(rb + 1) * FFN_ROWS, :] += jnp.dot(act, wo, preferred_element_type=_F32)

        pending = gate_up(0)
        for rb in range(1, n_rb):
            nxt = gate_up(rb)
            down(rb - 1, *pending)
            pending = nxt
        down(n_rb - 1, *pending)

    @pl.when(j >= nf)
    def _():
        slice_rows = x1_ref.shape[0]
        base = (j - nf) * slice_rows

        def resid_group(chunks):
            out_chunks = [pl.ds(pl.multiple_of(base + rows.start, ROW_CHUNK), ROW_CHUNK) for rows in chunks]
            scales = [_rms_scale(o_ref[rows, :]) for rows in out_chunks]
            for rows, out_rows, scale in zip(chunks, out_chunks, scales):
                o_ref[out_rows, :] = x1_ref[rows, :] + o_ref[out_rows, :] * scale * gpost_ref[...]
        _for_row_groups(slice_rows, resid_group)


def _ffn(h2, x1, w_in, w_out, gpost, *, bm, bf, n_epi):
    t, d = h2.shape
    f = w_out.shape[0]
    nf = f // bf
    assert t % bm == 0 and f % bf == 0 and w_in.shape == (d, 2 * f) and bm % FFN_ROWS == 0
    assert bm % n_epi == 0 and (bm // n_epi) % GROUP_ROWS == 0
    last = nf - 1
    return pl.pallas_call(
        partial(_ffn_kernel, nf=nf),
        grid=(t // bm, nf + n_epi),
        in_specs=[pl.BlockSpec((bm, d), lambda i, j: (i, 0), pipeline_mode=_SINGLE),
                  pl.BlockSpec((bm // n_epi, d),
                               lambda i, j: (i * n_epi + jnp.clip(j - nf, 0, n_epi - 1), 0)),
                  pl.BlockSpec((d, bf), lambda i, j: (0, jnp.minimum(j, last))),
                  pl.BlockSpec((d, bf), lambda i, j: (0, nf + jnp.minimum(j, last))),
                  pl.BlockSpec((bf, d), lambda i, j: (jnp.minimum(j, last), 0)),
                  pl.BlockSpec((1, d), lambda i, j: (0, 0))],
        out_specs=pl.BlockSpec((bm, d), lambda i, j: (i, 0), pipeline_mode=_SINGLE),
        out_shape=jax.ShapeDtypeStruct((t, d), _F32),
        compiler_params=_params(("arbitrary", "arbitrary")),
        name="swiglu_ffn",
    )(h2, x1, w_in, w_in, w_out, gpost)


def kernel(x, pre_mix_g, w_in, gmlp_v_norm_g, w_spatial, b_spatial, w_conv, b_conv, w_r, b_r, w_i, b_i, lru_lambda, out_norm_a_g, out_norm_b_g, w_out, post_mix_g, pre_ffn_g, w_ffn_in, w_ffn_out, post_ffn_g):
    batch, seq, d = x.shape
    depth = w_in.shape[0]
    xt = x.reshape(batch * seq, d)
    for l in range(depth):
        vec_rows = [gmlp_v_norm_g[l][None], w_conv[l], b_conv[l][None], b_r[l][None], b_i[l][None],
                    lru_lambda[l][None], out_norm_a_g[l][None], out_norm_b_g[l][None]]
        vecs = jnp.concatenate(vec_rows, axis=0)
        vecs = jnp.pad(vecs, ((0, _VEC_ROWS - vecs.shape[0]), (0, 0)))
        wri = jnp.concatenate([w_r[l], w_i[l]], axis=-1).astype(_BF16)
        ya, yb, qa, qb, w_out16 = _in_mix(
            xt, pre_mix_g[l][None], w_in[l].astype(_BF16), vecs, w_spatial[l],
            b_spatial[l][:, :, None], wri, w_out[l], seq=seq, bm=IN_BM, gw=IN_GW)
        x1, h2 = _out_proj(ya, yb, qa, qb, w_out16, xt, post_mix_g[l][None],
                           pre_ffn_g[l][None], bm=OUT_BM, bn=OUT_BN)
        xt = _ffn(h2, x1, w_ffn_in[l], w_ffn_out[l], post_ffn_g[l][None],
                  bm=FFN_BM, bf=FFN_BF, n_epi=FFN_EPI_SLICES)
    return xt.reshape(batch, seq, d)
```
